```python
import jax, jax.numpy as jnp
from jax import lax
import numpy as np

D_MODEL = 1024
BATCH = 8
SEQ = 8192
DEPTH = 2

CHUNK = 64
N_MIXERS = 2
N_LAYERS_A = (DEPTH + 1) // 2
N_LAYERS_B = DEPTH // 2
RMS_EPS = 1e-6
D_FF = 2816
A_HEADS = 16
A_Q_LORA = 256
A_KV_LORA = 256
A_NOPE = 64
A_ROPE = 32
A_V = 64
IDX_HEADS = 4
IDX_DIM = 64
IDX_ROPE = 32
TOPK_MAX = 256
Q_BLOCK = 128
ROPE_THETA = 10000.0
A_IN_COLS = A_Q_LORA + A_KV_LORA + A_ROPE + IDX_DIM + IDX_HEADS
B_QK_HEADS = 8
B_V_HEADS = 16
B_HEAD_DIM = 128
B_CONV = 4
B_QK_WIDTH = B_QK_HEADS * B_HEAD_DIM
B_V_WIDTH = B_V_HEADS * B_HEAD_DIM
B_CONV_WIDTH = 2 * B_QK_WIDTH + B_V_WIDTH
B_IN_COLS = B_CONV_WIDTH + B_V_WIDTH + 2 * B_V_HEADS
MEM_LEN = 256
X_HEADS = 4
X_HEAD_DIM = D_MODEL // X_HEADS

kernel_name = "hybrid_dsa_gdn_macaron_encoder"


def rms_norm(x, g):
    xf = x.astype(jnp.float32)
    y = xf * lax.rsqrt(jnp.mean(xf * xf, axis=-1, keepdims=True) + RMS_EPS)
    return (y * g.astype(jnp.float32)).astype(x.dtype)


def layer_norm(x, g, b):
    xf = x.astype(jnp.float32)
    mu = jnp.mean(xf, axis=-1, keepdims=True)
    var = jnp.mean(jnp.square(xf - mu), axis=-1, keepdims=True)
    y = (xf - mu) * lax.rsqrt(var + RMS_EPS)
    return (y * g.astype(jnp.float32) + b.astype(jnp.float32)).astype(x.dtype)


def l2_normalize(x):
    xf = x.astype(jnp.float32)
    return xf * lax.rsqrt(jnp.sum(xf * xf, axis=-1, keepdims=True) + RMS_EPS)


def rope_tables(positions, dim, dtype):
    inv = ROPE_THETA ** (-jnp.arange(0, dim, 2, dtype=jnp.float32) / dim)
    ang = positions.astype(jnp.float32)[..., None] * inv
    return jnp.cos(ang).astype(dtype), jnp.sin(ang).astype(dtype)


def rope(x, cos, sin):
    x1, x2 = jnp.split(x, 2, axis=-1)
    return jnp.concatenate([x1 * cos - x2 * sin, x1 * sin + x2 * cos], axis=-1)


def swiglu_ffn(h, w_in, w_out):
    gate, up = jnp.split(h @ w_in, 2, axis=-1)
    return (jax.nn.silu(gate) * up) @ w_out


def dsa_mixer(h, positions, w_in, norm_q, norm_kv, kidx_g, kidx_b, w_uq, w_uk, w_uv, w_qidx, w_o):
    bsz, seq, _ = h.shape
    dt = h.dtype
    offs = [A_Q_LORA, A_Q_LORA + A_KV_LORA, A_Q_LORA + A_KV_LORA + A_ROPE,
            A_Q_LORA + A_KV_LORA + A_ROPE + IDX_DIM]
    c_q, c_kv, k_rope, k_idx, w_idx = jnp.split(h @ w_in, offs, axis=-1)
    c_q = rms_norm(c_q, norm_q)
    c_kv = rms_norm(c_kv, norm_kv)
    cos, sin = rope_tables(positions, A_ROPE, dt)
    k_rope = rope(k_rope, cos, sin)
    k_idx = layer_norm(k_idx, kidx_g, kidx_b)
    k_idx = jnp.concatenate([rope(k_idx[..., :IDX_ROPE], cos, sin), k_idx[..., IDX_ROPE:]], axis=-1)
    w_idx = w_idx * (IDX_HEADS ** -0.5)
    kv_cat = jnp.concatenate([c_kv, k_rope], axis=-1)
    topk = min(TOPK_MAX, seq // 4)
    nb = seq // Q_BLOCK
    key_pos = jnp.arange(seq)
    scale = (A_NOPE + A_ROPE) ** -0.5

    def blocks(t):
        return jnp.moveaxis(t.reshape(bsz, nb, Q_BLOCK, *t.shape[2:]), 1, 0)

    def one_block(args):
        cq_b, cos_b, sin_b, widx_b, blk = args
        q_pos = blk * Q_BLOCK + jnp.arange(Q_BLOCK)
        limit = (q_pos // CHUNK + 1) * CHUNK
        cos_h, sin_h = cos_b[:, :, None, :], sin_b[:, :, None, :]
        q_idx = (cq_b @ w_qidx).reshape(bsz, Q_BLOCK, IDX_HEADS, IDX_DIM)
        q_idx = jnp.concatenate([rope(q_idx[..., :IDX_ROPE], cos_h, sin_h), q_idx[..., IDX_ROPE:]], axis=-1)
        logits = jnp.einsum('bthd,bsd->bths', q_idx, k_idx)
        score = jnp.einsum('bth,bths->bts', widx_b, jax.nn.relu(logits)).astype(jnp.float32) * (IDX_DIM ** -0.5)
        admissible = key_pos[None, :] < limit[:, None]
        score = jnp.where(admissible[None], score, -jnp.inf)
        _, sel = lax.top_k(score, topk)
        sel_ok = sel < limit[None, :, None]
        kv_sel = jax.vmap(lambda kv, i: kv[i])(kv_cat, sel)
        lat_sel, kr_sel = kv_sel[..., :A_KV_LORA], kv_sel[..., A_KV_LORA:]
        q = (cq_b @ w_uq).reshape(bsz, Q_BLOCK, A_HEADS, A_NOPE + A_ROPE)
        q_nope, q_pe = q[..., :A_NOPE], rope(q[..., A_NOPE:], cos_h, sin_h)
        q_lat = jnp.einsum('bthd,hcd->bthc', q_nope, w_uk)
        s = (jnp.einsum('bthc,btkc->bthk', q_lat, lat_sel)
             + jnp.einsum('bthr,btkr->bthk', q_pe, kr_sel)).astype(jnp.float32) * scale
        s = jnp.where(sel_ok[:, :, None, :], s, -jnp.inf)
        p = jax.nn.softmax(s, axis=-1).astype(dt)
        o_lat = jnp.einsum('bthk,btkc->bthc', p, lat_sel)
        o = jnp.einsum('bthc,hcv->bthv', o_lat, w_uv)
        return o.reshape(bsz, Q_BLOCK, A_HEADS * A_V)

    out = lax.map(one_block, (blocks(c_q), blocks(cos), blocks(sin), blocks(w_idx), jnp.arange(nb)))
    out = jnp.moveaxis(out, 0, 1).reshape(bsz, seq, A_HEADS * A_V)
    return out @ w_o


def causal_depthwise_conv(x, w):
    k = w.shape[0]
    return lax.conv_general_dilated(x, w[:, None, :].astype(x.dtype), window_strides=(1,),
                                    padding=[(k - 1, 0)], dimension_numbers=('NWC', 'WIO', 'NWC'),
                                    feature_group_count=x.shape[-1])


def chunked_gated_delta_rule(q, k, v, beta, g):
    bsz, seq, nh, dk = q.shape
    dv = v.shape[-1]
    n = seq // CHUNK

    def chunks(t):
        return jnp.moveaxis(t.reshape(bsz, n, CHUNK, nh, *t.shape[3:]), 3, 1)

    q, k, v, beta, g = (chunks(t) for t in (q, k, v, beta, g))
    g_cum = jnp.cumsum(g, axis=-1)
    idx = jnp.arange(CHUNK)
    causal = idx[:, None] >= idx[None, :]
    strict = idx[:, None] > idx[None, :]
    decay = jnp.exp(jnp.where(causal, g_cum[..., :, None] - g_cum[..., None, :], -jnp.inf))
    k_beta = k * beta[..., None]
    lower = jnp.where(strict, jnp.einsum('bhnid,bhnjd->bhnij', k_beta, k) * decay, 0.0)
    a_mat = jnp.eye(CHUNK, dtype=q.dtype) + lower
    u = lax.linalg.triangular_solve(a_mat, v * beta[..., None], left_side=True, lower=True, unit_diagonal=True)
    w = lax.linalg.triangular_solve(a_mat, k_beta * jnp.exp(g_cum)[..., None], left_side=True, lower=True,
                                    unit_diagonal=True)
    intra = jnp.einsum('bhnid,bhnjd->bhnij', q, k) * decay

    def step(state, xs):
        qc, kc, uc, wc, gc, ac = xs
        v_new = uc - jnp.einsum('bhck,bhkv->bhcv', wc, state)
        o = (jnp.einsum('bhck,bhkv->bhcv', qc * jnp.exp(gc)[..., None], state)
             + jnp.einsum('bhij,bhjv->bhiv', ac, v_new))
        g_last = gc[..., -1:]
        state = (state * jnp.exp(g_last)[..., None]
                 + jnp.einsum('bhck,bhcv->bhkv', kc * jnp.exp(g_last - gc)[..., None], v_new))
        return state, o

    xs = tuple(jnp.moveaxis(t, 2, 0) for t in (q, k, u, w, g_cum, intra))
    state0 = jnp.zeros((bsz, nh, dk, dv), q.dtype)
    _, o = lax.scan(step, state0, xs)
    o = jnp.moveaxis(o, 0, 2)
    return jnp.moveaxis(o, 1, 3).reshape(bsz, seq, nh, dv)


def gated_deltanet_mixer(h, w_in, conv_w, a_log, dt_bias, norm_o, w_o):
    bsz, seq, _ = h.shape
    dt = h.dtype
    offs = [B_CONV_WIDTH, B_CONV_WIDTH + B_V_WIDTH, B_CONV_WIDTH + B_V_WIDTH + B_V_HEADS]
    qkv, z, b, a = jnp.split(h @ w_in, offs, axis=-1)
    qkv = jax.nn.silu(causal_depthwise_conv(qkv, conv_w))
    q, k, v = jnp.split(qkv, [B_QK_WIDTH, 2 * B_QK_WIDTH], axis=-1)
    rep = B_V_HEADS // B_QK_HEADS
    q = jnp.repeat(l2_normalize(q.reshape(bsz, seq, B_QK_HEADS, B_HEAD_DIM)), rep, axis=2) * (B_HEAD_DIM ** -0.5)
    k = jnp.repeat(l2_normalize(k.reshape(bsz, seq, B_QK_HEADS, B_HEAD_DIM)), rep, axis=2)
    v = v.reshape(bsz, seq, B_V_HEADS, B_HEAD_DIM).astype(jnp.float32)
    beta = jax.nn.sigmoid(b.astype(jnp.float32))
    g = -jnp.exp(a_log.astype(jnp.float32)) * jax.nn.softplus(a.astype(jnp.float32) + dt_bias.astype(jnp.float32))
    o = chunked_gated_delta_rule(q, k, v, beta, g)
    o = rms_norm(o, norm_o) * jax.nn.silu(z.reshape(bsz, seq, B_V_HEADS, B_HEAD_DIM).astype(jnp.float32))
    return o.reshape(bsz, seq, B_V_WIDTH).astype(dt) @ w_o


def memory_cross_attention(h, mem_n, w_q, w_kv, w_o):
    bsz, seq, _ = h.shape
    q = (h @ w_q).reshape(bsz, seq, X_HEADS, X_HEAD_DIM)
    k, v = jnp.split(mem_n @ w_kv, 2, axis=-1)
    k = k.reshape(bsz, -1, X_HEADS, X_HEAD_DIM)
    v = v.reshape(bsz, -1, X_HEADS, X_HEAD_DIM)
    s = jnp.einsum('bthd,bmhd->bhtm', q, k).astype(jnp.float32) * (X_HEAD_DIM ** -0.5)
    p = jax.nn.softmax(s, axis=-1).astype(h.dtype)
    o = jnp.einsum('bhtm,bmhd->bthd', p, v).reshape(bsz, seq, D_MODEL)
    return o @ w_o


def setup_inputs(seed: int = 0) -> dict:
    key = jax.random.key(seed)
    ks = iter(jax.random.split(key, 48))

    def dense(shape, fan_in):
        return jax.random.normal(next(ks), shape, jnp.float32) * (fan_in ** -0.5)

    def gain(shape):
        return 1.0 + 0.02 * jax.random.normal(next(ks), shape, jnp.float32)

    def small(shape):
        return 0.02 * jax.random.normal(next(ks), shape, jnp.float32)

    x = jax.random.normal(next(ks), (BATCH, SEQ, D_MODEL), jnp.float32)
    mem = jax.random.normal(next(ks), (BATCH, MEM_LEN, D_MODEL), jnp.float32)
    offset = jax.random.randint(next(ks), (BATCH, 1), 0, 4096, dtype=jnp.int32)
    positions = offset + jnp.arange(SEQ, dtype=jnp.int32)[None, :]
    dt0 = jnp.exp(jax.random.uniform(next(ks), (N_LAYERS_B, B_V_HEADS), jnp.float32,
                                     np.log(1e-3), np.log(1e-1)))
    return {
        "x": x, "mem": mem, "positions": positions,
        "ffn1_norm": gain((DEPTH, D_MODEL)),
        "ffn1_w_in": dense((DEPTH, D_MODEL, 2 * D_FF), D_MODEL),
        "ffn1_w_out": dense((DEPTH, D_FF, D_MODEL), D_FF),
        "mix_norm": gain((DEPTH, D_MODEL)),
        "xattn_norm": gain((DEPTH, D_MODEL)),
        "mem_norm": gain((DEPTH, D_MODEL)),
        "xattn_w_q": dense((DEPTH, D_MODEL, D_MODEL), D_MODEL),
        "xattn_w_kv": dense((DEPTH, D_MODEL, 2 * D_MODEL), D_MODEL),
        "xattn_w_o": dense((DEPTH, D_MODEL, D_MODEL), D_MODEL),
        "ffn2_norm": gain((DEPTH, D_MODEL)),
        "ffn2_w_in": dense((DEPTH, D_MODEL, 2 * D_FF), D_MODEL),
        "ffn2_w_out": dense((DEPTH, D_FF, D_MODEL), D_FF),
        "a_w_in": dense((N_LAYERS_A, D_MODEL, A_IN_COLS), D_MODEL),
        "a_norm_q": gain((N_LAYERS_A, A_Q_LORA)),
        "a_norm_kv": gain((N_LAYERS_A, A_KV_LORA)),
        "a_kidx_g": gain((N_LAYERS_A, IDX_DIM)),
        "a_kidx_b": small((N_LAYERS_A, IDX_DIM)),
        "a_w_uq": dense((N_LAYERS_A, A_Q_LORA, A_HEADS * (A_NOPE + A_ROPE)), A_Q_LORA),
        "a_w_uk": dense((N_LAYERS_A, A_HEADS, A_KV_LORA, A_NOPE), A_KV_LORA),
        "a_w_uv": dense((N_LAYERS_A, A_HEADS, A_KV_LORA, A_V), A_KV_LORA),
        "a_w_qidx": dense((N_LAYERS_A, A_Q_LORA, IDX_HEADS * IDX_DIM), A_Q_LORA),
        "a_w_o": dense((N_LAYERS_A, A_HEADS * A_V, D_MODEL), A_HEADS * A_V),
        "b_w_in": dense((N_LAYERS_B, D_MODEL, B_IN_COLS), D_MODEL),
        "b_conv": dense((N_LAYERS_B, B_CONV, B_CONV_WIDTH), B_CONV),
        "b_a_log": jnp.log(jax.random.uniform(next(ks), (N_LAYERS_B, B_V_HEADS), jnp.float32, 1.0, 16.0)),
        "b_dt_bias": dt0 + jnp.log(-jnp.expm1(-dt0)),
        "b_norm_o": gain((N_LAYERS_B, B_HEAD_DIM)),
        "b_w_o": dense((N_LAYERS_B, B_V_WIDTH, D_MODEL), B_V_WIDTH),
        "final_norm": gain((D_MODEL,)),
    }


def reference(x, mem, positions, ffn1_norm, ffn1_w_in, ffn1_w_out, mix_norm, xattn_norm, mem_norm,
              xattn_w_q, xattn_w_kv, xattn_w_o, ffn2_norm, ffn2_w_in, ffn2_w_out,
              a_w_in, a_norm_q, a_norm_kv, a_kidx_g, a_kidx_b, a_w_uq, a_w_uk, a_w_uv, a_w_qidx, a_w_o,
              b_w_in, b_conv, b_a_log, b_dt_bias, b_norm_o, b_w_o, final_norm):
    for i in range(DEPTH):
        x = x + 0.5 * swiglu_ffn(rms_norm(x, ffn1_norm[i]), ffn1_w_in[i], ffn1_w_out[i])
        h = rms_norm(x, mix_norm[i])
        j = i // N_MIXERS
        if i % N_MIXERS == 0:
            x = x + dsa_mixer(h, positions, a_w_in[j], a_norm_q[j], a_norm_kv[j], a_kidx_g[j], a_kidx_b[j],
                              a_w_uq[j], a_w_uk[j], a_w_uv[j], a_w_qidx[j], a_w_o[j])
        else:
            x = x + gated_deltanet_mixer(h, b_w_in[j], b_conv[j], b_a_log[j], b_dt_bias[j], b_norm_o[j], b_w_o[j])
        x = x + memory_cross_attention(rms_norm(x, xattn_norm[i]), rms_norm(mem, mem_norm[i]),
                                       xattn_w_q[i], xattn_w_kv[i], xattn_w_o[i])
        x = x + 0.5 * swiglu_ffn(rms_norm(x, ffn2_norm[i]), ffn2_w_in[i], ffn2_w_out[i])
    return rms_norm(x, final_norm)
```

```python
import functools
import math

import jax
import jax.numpy as jnp
import numpy as np
from jax import lax
from jax.experimental import pallas as pl
from jax.experimental.pallas import tpu as pltpu

F32 = jnp.float32
BF16 = jnp.bfloat16
I32 = jnp.int32

RMS_EPS = 1e-6
ROPE_THETA = 10000.0
CHUNK = 64
TOPK_MAX = 256
A_HEADS = 16
A_LORA = 256
A_NOPE = 64
A_ROPE = 32
A_V = 64
IDX_HEADS = 4
IDX_DIM = 64
KV_W = 384
B_QK_HEADS = 8
B_V_HEADS = 16
B_HEAD = 128
B_CONV = 4
X_HEADS = 4
NEG_BIG = -1e30
INT_MIN = -2147483648
KEY_NEG_INF = -2139095041

VMEM_LIMIT_BYTES = 58 * 1024 * 1024


def _cparams(*sem):
    return pltpu.CompilerParams(dimension_semantics=sem, vmem_limit_bytes=VMEM_LIMIT_BYTES)


def _const_spec(shape):
    nd = len(shape)
    return pl.BlockSpec(shape, lambda *_: (0,) * nd, pipeline_mode=pl.Buffered(1))


def _rms(x, g):
    return x * lax.rsqrt(jnp.mean(x * x, axis=-1, keepdims=True) + RMS_EPS) * g


def _dot(a, b):
    return jnp.dot(a, b, preferred_element_type=F32)


def _dot_nt(a, b):
    return lax.dot_general(a, b, (((1,), (1,)), ((), ())), preferred_element_type=F32)


def _silu(x):
    return x * jax.nn.sigmoid(x)


def _ffn_kernel(x_ref, g_ref, win_ref, wout_ref, fg_ref, o_ref, acc_ref, *, d_ff, f_chunk, final_norm):
    x = x_ref[...]
    h = _rms(x, g_ref[...]).astype(BF16)
    for c in range(d_ff // f_chunk):
        lo = c * f_chunk
        gate = _dot(h, win_ref[:, lo:lo + f_chunk])
        up = _dot(h, win_ref[:, d_ff + lo:d_ff + lo + f_chunk])
        a = (_silu(gate) * up).astype(BF16)
        y = _dot(a, wout_ref[lo:lo + f_chunk, :])
        if c == 0:
            acc_ref[...] = y
        else:
            acc_ref[...] += y
    out = x + 0.5 * acc_ref[...]
    if final_norm:
        out = _rms(out, fg_ref[...])
    o_ref[...] = out


def _ffn(x2, g, w_in, w_out, final_g=None, tm=512):
    n, d = x2.shape
    d_ff = w_out.shape[0]
    f_chunk = 256
    assert n % tm == 0 and d_ff % f_chunk == 0
    fg = final_g if final_g is not None else g
    kern = functools.partial(_ffn_kernel, d_ff=d_ff, f_chunk=f_chunk, final_norm=final_g is not None)
    return pl.pallas_call(
        kern,
        grid=(n // tm,),
        in_specs=[
            pl.BlockSpec((tm, d), lambda i: (i, 0)),
            _const_spec((1, d)),
            _const_spec((d, 2 * d_ff)),
            _const_spec((d_ff, d)),
            _const_spec((1, d)),
        ],
        out_specs=pl.BlockSpec((tm, d), lambda i: (i, 0)),
        out_shape=jax.ShapeDtypeStruct((n, d), F32),
        scratch_shapes=[pltpu.VMEM((tm, d), F32)],
        compiler_params=_cparams("parallel"),
        name="ffn",
    )(x2, g.reshape(1, d), w_in.astype(BF16), w_out.astype(BF16), fg.reshape(1, d))


def _mem_kv_kernel(mem_ref, g_ref, wkv_ref, k_ref, v_ref, *, d):
    m = _rms(mem_ref[0], g_ref[...]).astype(BF16)
    kv = _dot(m, wkv_ref[...])
    k_ref[0] = kv[:, :d].astype(BF16)
    v_ref[0] = kv[:, d:].astype(BF16)


def _xattn_kernel(x_ref, g_ref, wq_ref, k_ref, v_ref, wo_ref, o_ref, *, heads):
    x = x_ref[0]
    d = x.shape[-1]
    hd = d // heads
    h = _rms(x, g_ref[...]).astype(BF16)
    q = (_dot(h, wq_ref[...]) * (hd ** -0.5)).astype(BF16)
    outs = []
    for i in range(heads):
        s = _dot_nt(q[:, i * hd:(i + 1) * hd], k_ref[0, :, i * hd:(i + 1) * hd])
        m = jnp.max(s, axis=-1, keepdims=True)
        p = jnp.exp(s - m)
        l = jnp.sum(p, axis=-1, keepdims=True)
        o = _dot(p.astype(BF16), v_ref[0, :, i * hd:(i + 1) * hd])
        outs.append((o / l).astype(BF16))
    o = jnp.concatenate(outs, axis=-1)
    o_ref[0] = x + _dot(o, wo_ref[...])


def _xattn(x, mem, g_x, g_mem, w_q, w_kv, w_o, tm=512):
    b, t, d = x.shape
    ml = mem.shape[1]
    k, v = pl.pallas_call(
        functools.partial(_mem_kv_kernel, d=d),
        grid=(b,),
        in_specs=[
            pl.BlockSpec((1, ml, d), lambda i: (i, 0, 0)),
            _const_spec((1, d)),
            _const_spec((d, 2 * d)),
        ],
        out_specs=[pl.BlockSpec((1, ml, d), lambda i: (i, 0, 0))] * 2,
        out_shape=[jax.ShapeDtypeStruct((b, ml, d), BF16)] * 2,
        compiler_params=_cparams("parallel"),
        name="mem_kv",
    )(mem, g_mem.reshape(1, d), w_kv.astype(BF16))
    tm = min(tm, t)
    return pl.pallas_call(
        functools.partial(_xattn_kernel, heads=X_HEADS),
        grid=(b, t // tm),
        in_specs=[
            pl.BlockSpec((1, tm, d), lambda i, j: (i, j, 0)),
            _const_spec((1, d)),
            _const_spec((d, d)),
            pl.BlockSpec((1, ml, d), lambda i, j: (i, 0, 0)),
            pl.BlockSpec((1, ml, d), lambda i, j: (i, 0, 0)),
            _const_spec((d, d)),
        ],
        out_specs=pl.BlockSpec((1, tm, d), lambda i, j: (i, j, 0)),
        out_shape=jax.ShapeDtypeStruct((b, t, d), F32),
        compiler_params=_cparams("parallel", "parallel"),
        name="xattn",
    )(x, g_x.reshape(1, d), w_q.astype(BF16), k, v, w_o.astype(BF16))


def _fold_nt_kernel(a_ref, b_ref, o_ref, *, scale):
    o = lax.dot_general(a_ref[0], b_ref[0], (((1,), (1,)), ((), ())),
                        preferred_element_type=F32, precision=lax.Precision.HIGHEST)
    o_ref[...] = (o * scale).astype(o_ref.dtype)


def _fold_nn_kernel(a_ref, b_ref, o_ref):
    o = jnp.dot(a_ref[0], b_ref[0], preferred_element_type=F32, precision=lax.Precision.HIGHEST)
    o_ref[0] = o.astype(o_ref.dtype)


def _fold_weights(w_uq, w_uk, w_uv, w_o, scale):
    nh = A_HEADS
    uq = w_uq.reshape(A_LORA, nh, A_NOPE + A_ROPE)
    uq_nope = jnp.transpose(uq[:, :, :A_NOPE], (1, 0, 2))
    w_qlat = pl.pallas_call(
        functools.partial(_fold_nt_kernel, scale=scale),
        grid=(nh,),
        in_specs=[pl.BlockSpec((1, A_LORA, A_NOPE), lambda h: (h, 0, 0)),
                  pl.BlockSpec((1, A_LORA, A_NOPE), lambda h: (h, 0, 0))],
        out_specs=pl.BlockSpec((A_LORA, A_LORA), lambda h: (0, h)),
        out_shape=jax.ShapeDtypeStruct((A_LORA, nh * A_LORA), BF16),
        compiler_params=_cparams("parallel"),
        name="fold_qlat",
    )(uq_nope, w_uk)
    d = w_o.shape[1]
    w_ov = pl.pallas_call(
        _fold_nn_kernel,
        grid=(nh,),
        in_specs=[pl.BlockSpec((1, A_LORA, A_V), lambda h: (h, 0, 0)),
                  pl.BlockSpec((1, A_V, d), lambda h: (h, 0, 0))],
        out_specs=pl.BlockSpec((1, A_LORA, d), lambda h: (h, 0, 0)),
        out_shape=jax.ShapeDtypeStruct((nh, A_LORA, d), BF16),
        compiler_params=_cparams("parallel"),
        name="fold_ov",
    )(w_uv, w_o.reshape(nh, A_V, d))
    return w_qlat, w_ov


def _dsa_proj_kernel(x_ref, pos_ref, g_ref, wa_ref, nq_ref, nkv_ref, lng_ref, lnb_ref, inv64_ref, inv32_ref,
                     wqi_ref, wpe_ref, wql_ref,
                     kv_ref, kidx_ref, qidx_ref, widx_ref, q_ref):
    x = x_ref[0]
    tm = x.shape[0]
    h = _rms(x, g_ref[...]).astype(BF16)
    p = _dot(h, wa_ref[...])
    cq = _rms(p[:, 0:256], nq_ref[...])
    ckv = _rms(p[:, 256:512], nkv_ref[...])
    posf = pos_ref[0].astype(F32)
    a64 = posf * inv64_ref[...]
    c64, s64 = jnp.cos(a64), jnp.sin(a64)
    a32 = posf * inv32_ref[...]
    c32, s32 = jnp.cos(a32), jnp.sin(a32)
    lane = lax.broadcasted_iota(I32, (tm, 128), 1)

    g2 = p[:, 512:640]
    kr = g2 * c32 + pltpu.roll(g2, 96, 1) * s32
    kr = jnp.where(lane < A_ROPE, kr, 0.0)
    kv_ref[0, :, 0:256] = ckv.astype(BF16)
    kv_ref[0, :, 256:384] = kr.astype(BF16)

    g3 = p[:, 640:768]
    valid = lane < IDX_DIM
    mu = jnp.sum(jnp.where(valid, g3, 0.0), axis=-1, keepdims=True) * (1.0 / IDX_DIM)
    dlt = g3 - mu
    var = jnp.sum(jnp.where(valid, dlt * dlt, 0.0), axis=-1, keepdims=True) * (1.0 / IDX_DIM)
    y = dlt * lax.rsqrt(var + RMS_EPS) * lng_ref[...] + lnb_ref[...]
    ki = y * c64 + pltpu.roll(y, 64, 1) * s64
    ki = jnp.where(valid, ki, pltpu.roll(ki, 64, 1))
    ki = ki.astype(BF16)
    kidx_ref[0] = jnp.concatenate([ki, ki], axis=-1)

    g4 = p[:, 768:896] * (IDX_HEADS ** -0.5)
    widx_ref[0] = g4.T[0:8, :]

    cqb = cq.astype(BF16)
    c64x2 = jnp.concatenate([c64, c64], axis=-1)
    s64x2 = jnp.concatenate([s64, s64], axis=-1)
    qi = _dot(cqb, wqi_ref[:, 0:256]) * c64x2 + _dot(cqb, wqi_ref[:, 256:512]) * s64x2
    qidx_ref[0] = qi.astype(BF16)

    c32x4 = jnp.concatenate([c32] * 4, axis=-1)
    s32x4 = jnp.concatenate([s32] * 4, axis=-1)
    qpe = _dot(cqb, wpe_ref[:, 0:512]) * c32x4 + _dot(cqb, wpe_ref[:, 512:1024]) * s32x4
    for hh in range(A_HEADS):
        ql = _dot(cqb, wql_ref[:, hh * 256:(hh + 1) * 256])
        q_ref[0, hh, :, 0:256] = ql.astype(BF16)
        tile = qpe[:, (hh // 4) * 128:(hh // 4 + 1) * 128]
        off = (hh % 4) * A_ROPE
        if off:
            tile = pltpu.roll(tile, 128 - off, 1)
        q_ref[0, hh, :, 256:384] = jnp.where(lane < A_ROPE, tile, 0.0).astype(BF16)


def _rot_half_cols(w, width):
    k = w.shape[0]
    wg = w.reshape(k, -1, width)
    half = A_ROPE // 2
    sw = jnp.concatenate([-wg[:, :, half:A_ROPE], wg[:, :, :half],
                          jnp.zeros((k, wg.shape[1], width - A_ROPE), w.dtype)], axis=-1)
    return sw.reshape(k, -1)


def _dsa_proj(x, positions, g_mix, w_in, norm_q, norm_kv, kidx_g, kidx_b, w_uq, w_qidx, w_qlat, scale, tm=256):
    b, t, d = x.shape
    tm = min(tm, t)
    half = A_ROPE // 2
    o_kr = 2 * A_LORA
    o_ki = o_kr + A_ROPE
    o_w = o_ki + IDX_DIM
    zeros = lambda n: jnp.zeros((d, n), F32)
    w_kr = w_in[:, o_kr:o_ki]
    w_ki = w_in[:, o_ki:o_w]
    w_ki_perm = jnp.concatenate([w_ki[:, half:A_ROPE], w_ki[:, :half]], axis=-1)
    wa = jnp.concatenate([
        w_in[:, :o_kr],
        w_kr, _rot_half_cols(w_kr, A_ROPE), zeros(64),
        w_ki, w_ki_perm, zeros(32),
        w_in[:, o_w:o_w + IDX_HEADS], zeros(128 - IDX_HEADS)], axis=-1).astype(BF16)
    sgn = jnp.concatenate([-jnp.ones((half,), F32), jnp.ones((half,), F32)])
    perm = lambda v: jnp.concatenate([v[half:A_ROPE], v[:half]])
    lng = jnp.concatenate([kidx_g, sgn * perm(kidx_g), jnp.zeros((32,), F32)]).reshape(1, 128)
    lnb = jnp.concatenate([kidx_b, sgn * perm(kidx_b), jnp.zeros((32,), F32)]).reshape(1, 128)
    inv = ROPE_THETA ** (-jnp.arange(0, A_ROPE, 2, dtype=F32) / A_ROPE)
    inv64 = jnp.tile(jnp.concatenate([inv, inv, jnp.zeros((32,), F32)]), 2).reshape(1, 128)
    inv32 = jnp.tile(inv, 8).reshape(1, 128)
    wqi = jnp.concatenate([w_qidx, _rot_half_cols(w_qidx, IDX_DIM)], axis=-1).astype(BF16)
    uq = w_uq.reshape(A_LORA, A_HEADS, A_NOPE + A_ROPE)
    w_pe = (uq[:, :, A_NOPE:] * scale).reshape(A_LORA, A_HEADS * A_ROPE)
    wpe = jnp.concatenate([w_pe, _rot_half_cols(w_pe, A_ROPE)], axis=-1).astype(BF16)
    row = lambda a: a.reshape(1, -1)
    outs = pl.pallas_call(
        _dsa_proj_kernel,
        grid=(b, t // tm),
        in_specs=[
            pl.BlockSpec((1, tm, d), lambda i, j: (i, j, 0)),
            pl.BlockSpec((1, tm, 1), lambda i, j: (i, j, 0)),
            _const_spec((1, d)),
            _const_spec(wa.shape),
            _const_spec((1, A_LORA)), _const_spec((1, A_LORA)),
            _const_spec((1, 128)), _const_spec((1, 128)), _const_spec((1, 128)), _const_spec((1, 128)),
            _const_spec(wqi.shape), _const_spec(wpe.shape), _const_spec(w_qlat.shape),
        ],
        out_specs=[
            pl.BlockSpec((1, tm, KV_W), lambda i, j: (i, j, 0)),
            pl.BlockSpec((1, tm, 256), lambda i, j: (i, j, 0)),
            pl.BlockSpec((1, tm, 256), lambda i, j: (i, j, 0)),
            pl.BlockSpec((1, 8, tm), lambda i, j: (i, 0, j)),
            pl.BlockSpec((1, A_HEADS, tm, KV_W), lambda i, j: (i, 0, j, 0)),
        ],
        out_shape=[
            jax.ShapeDtypeStruct((b, t, KV_W), BF16),
            jax.ShapeDtypeStruct((b, t, 256), BF16),
            jax.ShapeDtypeStruct((b, t, 256), BF16),
            jax.ShapeDtypeStruct((b, 8, t), F32),
            jax.ShapeDtypeStruct((b, A_HEADS, t, KV_W), BF16),
        ],
        compiler_params=_cparams("parallel", "parallel"),
        name="dsa_proj",
    )(x, positions.reshape(b, t, 1), row(g_mix), wa, row(norm_q), row(norm_kv), lng, lnb, inv64, inv32,
      wqi, wpe, w_qlat)
    return outs


def _dsa_attn_kernel(x_ref, q_ref, qidx_ref, widx_ref, kidx_ref, kv_ref, wov_ref, ltri_ref,
                     o_ref, keys_ref, m_ref, l_ref, acc_ref, *, tq, kb, topk):
    i = pl.program_id(1)
    t0 = i * tq
    nkb = (t0 + tq + kb - 1) // kb
    nh = A_HEADS
    rows = nh * tq

    lane_q = lax.broadcasted_iota(I32, (1, tq), 1)
    limit = ((t0 + lane_q) // CHUNK + 1) * CHUNK
    row_k = lax.broadcasted_iota(I32, (kb, tq), 0)

    qi = qidx_ref[0]
    lane256 = lax.broadcasted_iota(I32, (tq, 256), 1)
    qheads = [jnp.where((lane256 // IDX_DIM) == hh, qi, jnp.zeros_like(qi)) for hh in range(IDX_HEADS)]
    wrows = [widx_ref[0, hh:hh + 1, :] for hh in range(IDX_HEADS)]

    def score_block(j, carry):
        kblk = kidx_ref[0, pl.ds(pl.multiple_of(j * kb, kb), kb), :]
        sc = jnp.zeros((kb, tq), F32)
        for hh in range(IDX_HEADS):
            sc = sc + jnp.maximum(_dot_nt(kblk, qheads[hh]), 0.0) * wrows[hh]
        sc = sc * (IDX_DIM ** -0.5)
        bits = lax.bitcast_convert_type(sc, I32)
        bits = jnp.where(bits == INT_MIN, 0, bits)
        key = bits ^ ((bits >> 31) & 0x7FFFFFFF)
        adm = (j * kb + row_k) < limit
        keys_ref[j] = jnp.where(adm, key, KEY_NEG_INF)
        return carry

    lax.fori_loop(0, nkb, score_block, 0)

    def count(pred_fn):
        def body(j, c):
            hit = jnp.where(pred_fn(keys_ref[j]), 1, 0).astype(I32)
            return c + jnp.sum(hit.reshape(kb // 8, 8, tq), axis=0)
        c = lax.fori_loop(0, nkb, body, jnp.zeros((8, tq), I32))
        return jnp.sum(c, axis=0, keepdims=True)

    def bit_step(bi, thr):
        cand = thr + jnp.left_shift(jnp.int32(1), 31 - bi)
        cnt = count(lambda k: k >= cand)
        return jnp.where(cnt >= topk, cand, thr)

    thr = lax.fori_loop(0, 32, bit_step, jnp.full((1, tq), INT_MIN, I32))
    need = (topk - count(lambda k: k > thr)).astype(F32)

    m_ref[...] = jnp.full(m_ref.shape, -jnp.inf, F32)
    l_ref[...] = jnp.zeros(l_ref.shape, F32)
    acc_ref[...] = jnp.zeros(acc_ref.shape, F32)

    def attn_block(j, tie_carry):
        key = keys_ref[j]
        gt = key > thr
        eq = key == thr
        pref = _dot(ltri_ref[...], jnp.where(eq, 1.0, 0.0).astype(BF16))
        rank = tie_carry + pref
        adm = (j * kb + row_k) < limit
        sel = adm & (gt | (eq & (rank <= need)))
        bias = jnp.where(sel, 0.0, NEG_BIG).T
        kvb = kv_ref[0, pl.ds(pl.multiple_of(j * kb, kb), kb), :]
        q = q_ref[0].reshape(rows, KV_W)
        s = _dot_nt(q, kvb)
        s = (s.reshape(nh, tq, kb) + bias[None]).reshape(rows, kb)
        m_prev = m_ref[...]
        m_new = jnp.maximum(m_prev, jnp.max(s, axis=-1, keepdims=True))
        alpha = jnp.exp(m_prev - m_new)
        p = jnp.exp(s - m_new)
        l_ref[...] = alpha * l_ref[...] + jnp.sum(p, axis=-1, keepdims=True)
        acc_ref[...] = alpha * acc_ref[...] + _dot(p.astype(BF16), kvb[:, 0:A_LORA])
        m_ref[...] = m_new
        return tie_carry + pref[kb - 1:kb, :]

    lax.fori_loop(0, nkb, attn_block, jnp.zeros((1, tq), F32))

    o_lat = (acc_ref[...] / l_ref[...]).astype(BF16).reshape(nh, tq, A_LORA)
    y = x_ref[0]
    for hh in range(nh):
        y = y + _dot(o_lat[hh], wov_ref[hh])
    o_ref[0] = y


def _dsa_attn(x, q, qidx, widx, kidx, kv, w_ov, tq=128, kb=256):
    b, t, d = x.shape
    topk = min(TOPK_MAX, t // 4)
    tq = min(tq, t)
    kb = min(kb, t)
    assert t % kb == 0 and t % tq == 0 and tq % CHUNK == 0
    ltri = jnp.asarray(np.tril(np.ones((kb, kb), np.float32)), BF16)
    rows = A_HEADS * tq
    kern = functools.partial(_dsa_attn_kernel, tq=tq, kb=kb, topk=topk)
    per_batch = lambda shape: pl.BlockSpec(shape, lambda i, j: (i, 0, 0), pipeline_mode=pl.Buffered(1))
    return pl.pallas_call(
        kern,
        grid=(b, t // tq),
        in_specs=[
            pl.BlockSpec((1, tq, d), lambda i, j: (i, j, 0)),
            pl.BlockSpec((1, A_HEADS, tq, KV_W), lambda i, j: (i, 0, j, 0)),
            pl.BlockSpec((1, tq, 256), lambda i, j: (i, j, 0)),
            pl.BlockSpec((1, 8, tq), lambda i, j: (i, 0, j)),
            per_batch((1, t, 256)),
            per_batch((1, t, KV_W)),
            _const_spec(w_ov.shape),
            _const_spec((kb, kb)),
        ],
        out_specs=pl.BlockSpec((1, tq, d), lambda i, j: (i, j, 0)),
        out_shape=jax.ShapeDtypeStruct((b, t, d), F32),
        scratch_shapes=[
            pltpu.VMEM((t // kb, kb, tq), I32),
            pltpu.VMEM((rows, 1), F32),
            pltpu.VMEM((rows, 1), F32),
            pltpu.VMEM((rows, A_LORA), F32),
        ],
        compiler_params=_cparams("parallel", "parallel"),
        name="dsa_attn",
    )(x, q, qidx, widx, kidx, kv, w_ov, ltri)


def _dsa_layer(x, positions, g_mix, w_in, norm_q, norm_kv, kidx_g, kidx_b, w_uq, w_uk, w_uv, w_qidx, w_o):
    scale = (A_NOPE + A_ROPE) ** -0.5
    w_qlat, w_ov = _fold_weights(w_uq, w_uk, w_uv, w_o, scale)
    kv, kidx, qidx, widx, q = _dsa_proj(x, positions, g_mix, w_in, norm_q, norm_kv, kidx_g, kidx_b,
                                        w_uq, w_qidx, w_qlat, scale)
    return _dsa_attn(x, q, qidx, widx, kidx, kv, w_ov)


def _gdn_proj_kernel(x_ref, g_ref, wqkv_ref, wz_ref, wba_ref, conv_ref, alog_ref, dtb_ref, lblk_ref,
                     q_ref, k_ref, v_ref, z_ref, bg_ref, buf_ref, *, tm, cw):
    @pl.when(pl.program_id(1) == 0)
    def _():
        buf_ref[0:8, :] = jnp.zeros((8, buf_ref.shape[1]), F32)

    x = x_ref[0]
    h = _rms(x, g_ref[...]).astype(BF16)
    qk_w = B_QK_HEADS * B_HEAD
    ncol = wqkv_ref.shape[1]
    for c in range(ncol // cw):
        lo = c * cw
        cur = _dot(h, wqkv_ref[:, lo:lo + cw])
        buf_ref[8:8 + tm, lo:lo + cw] = cur
        y = cur * conv_ref[3:4, lo:lo + cw]
        for tap in range(B_CONV - 1):
            sh = B_CONV - 1 - tap
            y = y + buf_ref[8 - sh:8 - sh + tm, lo:lo + cw] * conv_ref[tap:tap + 1, lo:lo + cw]
        buf_ref[0:8, lo:lo + cw] = buf_ref[tm:tm + 8, lo:lo + cw]
        y = _silu(y)
        for s in range(cw // B_HEAD):
            col = lo + s * B_HEAD
            yh = y[:, s * B_HEAD:(s + 1) * B_HEAD]
            if col < 2 * qk_w:
                yh = yh * lax.rsqrt(jnp.sum(yh * yh, axis=-1, keepdims=True) + RMS_EPS)
                if col < qk_w:
                    q_ref[0, :, col:col + B_HEAD] = (yh * (B_HEAD ** -0.5)).astype(BF16)
                else:
                    k_ref[0, :, col - qk_w:col - qk_w + B_HEAD] = yh.astype(BF16)
            else:
                v_ref[0, :, col - 2 * qk_w:col - 2 * qk_w + B_HEAD] = yh.astype(BF16)
    z_ref[0] = _dot(h, wz_ref[...]).astype(BF16)
    ba = _dot(h, wba_ref[...])
    beta = jax.nn.sigmoid(ba)
    sp_in = ba + dtb_ref[...]
    softplus = jnp.maximum(sp_in, 0.0) + jnp.log1p(jnp.exp(-jnp.abs(sp_in)))
    g = -jnp.exp(alog_ref[...]) * softplus
    gcum = jnp.dot(lblk_ref[...], g, preferred_element_type=F32, precision=lax.Precision.HIGHEST)
    lane = lax.broadcasted_iota(I32, (tm, 128), 1)
    bg_ref[0] = jnp.where(lane < B_V_HEADS, beta, gcum)


def _gdn_proj(x, g_mix, w_in, conv_w, a_log, dt_bias, tm=256):
    b, t, d = x.shape
    tm = min(tm, t)
    qk_w = B_QK_HEADS * B_HEAD
    v_w = B_V_HEADS * B_HEAD
    cwid = 2 * qk_w + v_w
    wqkv = w_in[:, :cwid].astype(BF16)
    wz = w_in[:, cwid:cwid + v_w].astype(BF16)
    wba = jnp.concatenate([w_in[:, cwid + v_w:], jnp.zeros((d, 128 - 2 * B_V_HEADS), F32)], axis=-1).astype(BF16)
    pad = lambda v: jnp.concatenate([jnp.zeros((B_V_HEADS,), F32), v, jnp.zeros((128 - 2 * B_V_HEADS,), F32)]).reshape(1, 128)
    blk = np.arange(tm) // CHUNK
    lblk = jnp.asarray(((blk[:, None] == blk[None, :]) & (np.arange(tm)[:, None] >= np.arange(tm)[None, :])).astype(np.float32))
    kern = functools.partial(_gdn_proj_kernel, tm=tm, cw=512)
    tok = lambda w: pl.BlockSpec((1, tm, w), lambda i, j: (i, j, 0))
    return pl.pallas_call(
        kern,
        grid=(b, t // tm),
        in_specs=[
            tok(d), _const_spec((1, d)), _const_spec(wqkv.shape), _const_spec(wz.shape), _const_spec(wba.shape),
            _const_spec((B_CONV, cwid)), _const_spec((1, 128)), _const_spec((1, 128)), _const_spec((tm, tm)),
        ],
        out_specs=[tok(qk_w), tok(qk_w), tok(v_w), tok(v_w), tok(128)],
        out_shape=[
            jax.ShapeDtypeStruct((b, t, qk_w), BF16),
            jax.ShapeDtypeStruct((b, t, qk_w), BF16),
            jax.ShapeDtypeStruct((b, t, v_w), BF16),
            jax.ShapeDtypeStruct((b, t, v_w), BF16),
            jax.ShapeDtypeStruct((b, t, 128), F32),
        ],
        scratch_shapes=[pltpu.VMEM((tm + 8, cwid), F32)],
        compiler_params=_cparams("parallel", "arbitrary"),
        name="gdn_proj",
    )(x, g_mix.reshape(1, d), wqkv, wz, wba, conv_w, pad(a_log), pad(dt_bias), lblk)


def _gdn_prep_kernel(q_ref, k_ref, v_ref, bg_ref, u_ref, w_ref, qg_ref, kd_ref, aqk_ref, eg_ref, *, tp):
    hv = pl.program_id(2)
    c = CHUNK
    ri = lax.broadcasted_iota(I32, (c, c), 0)
    ci = lax.broadcasted_iota(I32, (c, c), 1)
    eye = ri == ci
    lane = lax.broadcasted_iota(I32, (c, 128), 1)
    rowi = lax.broadcasted_iota(I32, (c, 1), 0)
    for n in range(tp // c):
        sl = slice(n * c, (n + 1) * c)
        bg = bg_ref[0, sl, :]
        bcol = jnp.sum(jnp.where(lane == hv, bg, 0.0), axis=-1, keepdims=True)
        gcol = jnp.sum(jnp.where(lane == hv + B_V_HEADS, bg, 0.0), axis=-1, keepdims=True)
        grow = jnp.sum(jnp.where(eye, gcol, 0.0), axis=0, keepdims=True)
        glast = jnp.sum(jnp.where(rowi == c - 1, gcol, 0.0), axis=0, keepdims=True)
        kb_ = k_ref[0, sl, :]
        qb_ = q_ref[0, sl, :]
        kf = kb_.astype(F32)
        decay = jnp.exp(jnp.where(ri >= ci, gcol - grow, -jnp.inf))
        kk = _dot_nt(kb_, kb_)
        x_ = jnp.where(ri > ci, -(bcol * kk * decay), 0.0)
        inv = jnp.where(eye, 1.0, 0.0) + x_
        xp = x_
        for _ in range(5):
            xpb = xp.astype(BF16)
            xp = _dot(xpb, xpb)
            inv = inv + _dot(inv.astype(BF16), xp.astype(BF16))
        rhs = jnp.concatenate([v_ref[0, sl, :].astype(F32) * bcol, kf * (bcol * jnp.exp(gcol))], axis=-1)
        uw = _dot(inv.astype(BF16), rhs.astype(BF16))
        u_ref[0, sl, :] = uw[:, :B_HEAD]
        w_ref[0, sl, :] = uw[:, B_HEAD:].astype(BF16)
        qg_ref[0, sl, :] = (qb_.astype(F32) * jnp.exp(gcol)).astype(BF16)
        kd_ref[0, sl, :] = (kf * jnp.exp(glast - gcol)).astype(BF16)
        aqk = jnp.where(ri >= ci, _dot_nt(qb_, kb_) * decay, 0.0)
        aqk_ref[0, sl, :] = jnp.concatenate([aqk, jnp.zeros_like(aqk)], axis=-1).astype(BF16)
        eg_ref[0, 0, n:n + 1, :] = jnp.broadcast_to(jnp.exp(glast), (1, 128))


def _gdn_prep(q, k, v, bg, tp=512):
    b, t, _ = q.shape
    tp = min(tp, t)
    nv = B_V_HEADS
    rep = B_V_HEADS // B_QK_HEADS
    v_w = nv * B_HEAD
    hspec = lambda: pl.BlockSpec((1, tp, B_HEAD), lambda i, j, h: (i, j, h))
    qkspec = lambda: pl.BlockSpec((1, tp, B_HEAD), lambda i, j, h: (i, j, h // rep))
    return pl.pallas_call(
        functools.partial(_gdn_prep_kernel, tp=tp),
        grid=(b, t // tp, nv),
        in_specs=[qkspec(), qkspec(), hspec(), pl.BlockSpec((1, tp, 128), lambda i, j, h: (i, j, 0))],
        out_specs=[hspec(), hspec(), hspec(), hspec(), hspec(),
                   pl.BlockSpec((1, 1, tp // CHUNK, 128), lambda i, j, h: (i, h, j, 0))],
        out_shape=[
            jax.ShapeDtypeStruct((b, t, v_w), F32),
            jax.ShapeDtypeStruct((b, t, v_w), BF16),
            jax.ShapeDtypeStruct((b, t, v_w), BF16),
            jax.ShapeDtypeStruct((b, t, v_w), BF16),
            jax.ShapeDtypeStruct((b, t, v_w), BF16),
            jax.ShapeDtypeStruct((b, nv, t // CHUNK, 128), F32),
        ],
        compiler_params=_cparams("parallel", "parallel", "parallel"),
        name="gdn_prep",
    )(q, k, v, bg)


def _gdn_scan_kernel(u_ref, w_ref, qg_ref, kd_ref, aqk_ref, eg_ref, z_ref, no_ref, o_ref, state_ref, *, tt, hb):
    @pl.when(pl.program_id(2) == 0)
    def _():
        state_ref[...] = jnp.zeros(state_ref.shape, F32)

    c = CHUNK

    def chunk_body(n, carry):
        r0 = pl.multiple_of(n * c, c)
        rs = pl.ds(r0, c)
        for hh in range(hb):
            ls = slice(hh * B_HEAD, (hh + 1) * B_HEAD)
            st = state_ref[hh]
            stb = st.astype(BF16)
            v_new = u_ref[0, rs, ls] - _dot(w_ref[0, rs, ls], stb)
            vb = v_new.astype(BF16)
            aqk = aqk_ref[0, rs, ls][:, 0:c]
            o = _dot(qg_ref[0, rs, ls], stb) + _dot(aqk, vb)
            eg = eg_ref[0, hh, pl.ds(n, 1), :]
            upd = lax.dot_general(kd_ref[0, rs, ls], vb, (((0,), (0,)), ((), ())), preferred_element_type=F32)
            state_ref[hh] = st * eg + upd
            on = o * lax.rsqrt(jnp.mean(o * o, axis=-1, keepdims=True) + RMS_EPS) * no_ref[...]
            o_ref[0, rs, ls] = (on * _silu(z_ref[0, rs, ls].astype(F32))).astype(BF16)
        return carry

    lax.fori_loop(0, tt // c, chunk_body, 0)


def _gdn_scan(u, w, qg, kd, aqk, eg, z, norm_o, tt=1024, hb=8):
    b, t, v_w = u.shape
    tt = min(tt, t)
    nv = B_V_HEADS
    wid = hb * B_HEAD
    spec = lambda: pl.BlockSpec((1, tt, wid), lambda i, g, j: (i, j, g))
    return pl.pallas_call(
        functools.partial(_gdn_scan_kernel, tt=tt, hb=hb),
        grid=(b, nv // hb, t // tt),
        in_specs=[spec(), spec(), spec(), spec(), spec(),
                  pl.BlockSpec((1, hb, tt // CHUNK, 128), lambda i, g, j: (i, g, j, 0)),
                  spec(), _const_spec((1, B_HEAD))],
        out_specs=spec(),
        out_shape=jax.ShapeDtypeStruct((b, t, v_w), BF16),
        scratch_shapes=[pltpu.VMEM((hb, B_HEAD, B_HEAD), F32)],
        compiler_params=_cparams("parallel", "parallel", "arbitrary"),
        name="gdn_scan",
    )(u, w, qg, kd, aqk, eg, z, norm_o.reshape(1, B_HEAD))


def _proj_res_kernel(x_ref, a_ref, w_ref, o_ref):
    o_ref[...] = x_ref[...] + _dot(a_ref[...], w_ref[...])


def _proj_res(x2, a2, w, tm=512):
    n, d = x2.shape
    kdim = a2.shape[1]
    tm = min(tm, n)
    return pl.pallas_call(
        _proj_res_kernel,
        grid=(n // tm,),
        in_specs=[pl.BlockSpec((tm, d), lambda i: (i, 0)), pl.BlockSpec((tm, kdim), lambda i: (i, 0)),
                  _const_spec((kdim, d))],
        out_specs=pl.BlockSpec((tm, d), lambda i: (i, 0)),
        out_shape=jax.ShapeDtypeStruct((n, d), F32),
        compiler_params=_cparams("parallel"),
        name="proj_res",
    )(x2, a2, w.astype(BF16))


def _gdn_layer(x, g_mix, w_in, conv_w, a_log, dt_bias, norm_o, w_o):
    b, t, d = x.shape
    q, k, v, z, bg = _gdn_proj(x, g_mix, w_in, conv_w, a_log, dt_bias)
    u, w, qg, kd, aqk, eg = _gdn_prep(q, k, v, bg)
    o = _gdn_scan(u, w, qg, kd, aqk, eg, z, norm_o)
    return _proj_res(x.reshape(b * t, d), o.reshape(b * t, -1), w_o).reshape(b, t, d)


def kernel(x, mem, positions, ffn1_norm, ffn1_w_in, ffn1_w_out, mix_norm, xattn_norm, mem_norm, xattn_w_q, xattn_w_kv, xattn_w_o, ffn2_norm, ffn2_w_in, ffn2_w_out, a_w_in, a_norm_q, a_norm_kv, a_kidx_g, a_kidx_b, a_w_uq, a_w_uk, a_w_uv, a_w_qidx, a_w_o, b_w_in, b_conv, b_a_log, b_dt_bias, b_norm_o, b_w_o, final_norm):
    b, t, d = x.shape
    depth = ffn1_norm.shape[0]
    n_mixers = 2
    for i in range(depth):
        x = _ffn(x.reshape(b * t, d), ffn1_norm[i], ffn1_w_in[i], ffn1_w_out[i]).reshape(b, t, d)
        j = i // n_mixers
        if i % n_mixers == 0:
            x = _dsa_layer(x, positions, mix_norm[i], a_w_in[j], a_norm_q[j], a_norm_kv[j], a_kidx_g[j],
                           a_kidx_b[j], a_w_uq[j], a_w_uk[j], a_w_uv[j], a_w_qidx[j], a_w_o[j])
        else:
            x = _gdn_layer(x, mix_norm[i], b_w_in[j], b_conv[j], b_a_log[j], b_dt_bias[j], b_norm_o[j], b_w_o[j])
        x = _xattn(x, mem, xattn_norm[i], mem_norm[i], xattn_w_q[i], xattn_w_kv[i], xattn_w_o[i])
        last = i == depth - 1
        x = _ffn(x.reshape(b * t, d), ffn2_norm[i], ffn2_w_in[i], ffn2_w_out[i],
                 final_g=final_norm if last else None).reshape(b, t, d)
    return x
```

```python
import functools
import math

import jax
import jax.numpy as jnp
import numpy as np
from jax import lax
from jax.experimental import pallas as pl
from jax.experimental.pallas import tpu as pltpu

F32 = jnp.float32
BF16 = jnp.bfloat16
I32 = jnp.int32
I16 = jnp.int16

RMS_EPS = 1e-6
ROPE_THETA = 10000.0
CHUNK = 64
TOPK_MAX = 256
A_HEADS = 16
A_LORA = 256
A_NOPE = 64
A_ROPE = 32
A_V = 64
IDX_HEADS = 4
IDX_DIM = 64
KV_W = 384
B_QK_HEADS = 8
B_V_HEADS = 16
B_HEAD = 128
B_CONV = 4
X_HEADS = 4
NEG_BIG = -1e30
INT_MIN = -2147483648
KEY_NEG_INF = -2139095041

VMEM_LIMIT_BYTES = 58 * 1024 * 1024


def _cparams(*sem):
    return pltpu.CompilerParams(dimension_semantics=sem, vmem_limit_bytes=VMEM_LIMIT_BYTES)


def _const_spec(shape):
    nd = len(shape)
    return pl.BlockSpec(shape, lambda *_: (0,) * nd, pipeline_mode=pl.Buffered(1))


def _rms(x, g):
    return x * lax.rsqrt(jnp.mean(x * x, axis=-1, keepdims=True) + RMS_EPS) * g


def _dot(a, b):
    return jnp.dot(a, b, preferred_element_type=F32)


def _dot_nt(a, b):
    return lax.dot_general(a, b, (((1,), (1,)), ((), ())), preferred_element_type=F32)


def _silu(x):
    return x * jax.nn.sigmoid(x)


def _ffn_kernel(x_ref, g_ref, win_ref, wout_ref, fg_ref, o_ref, acc_ref, *, d_ff, f_chunk, final_norm):
    x = x_ref[...]
    h = _rms(x, g_ref[...]).astype(BF16)
    for c in range(d_ff // f_chunk):
        lo = c * f_chunk
        gate = _dot(h, win_ref[:, lo:lo + f_chunk])
        up = _dot(h, win_ref[:, d_ff + lo:d_ff + lo + f_chunk])
        a = (_silu(gate) * up).astype(BF16)
        y = _dot(a, wout_ref[lo:lo + f_chunk, :])
        if c == 0:
            acc_ref[...] = y
        else:
            acc_ref[...] += y
    out = x + 0.5 * acc_ref[...]
    if final_norm:
        out = _rms(out, fg_ref[...])
    o_ref[...] = out


def _ffn(x2, g, w_in, w_out, final_g=None, tm=512):
    n, d = x2.shape
    d_ff = w_out.shape[0]
    f_chunk = 256
    assert n % tm == 0 and d_ff % f_chunk == 0
    fg = final_g if final_g is not None else g
    kern = functools.partial(_ffn_kernel, d_ff=d_ff, f_chunk=f_chunk, final_norm=final_g is not None)
    return pl.pallas_call(
        kern,
        grid=(n // tm,),
        in_specs=[
            pl.BlockSpec((tm, d), lambda i: (i, 0)),
            _const_spec((1, d)),
            _const_spec((d, 2 * d_ff)),
            _const_spec((d_ff, d)),
            _const_spec((1, d)),
        ],
        out_specs=pl.BlockSpec((tm, d), lambda i: (i, 0)),
        out_shape=jax.ShapeDtypeStruct((n, d), F32),
        scratch_shapes=[pltpu.VMEM((tm, d), F32)],
        compiler_params=_cparams("parallel"),
        name="ffn",
    )(x2, g.reshape(1, d), w_in.astype(BF16), w_out.astype(BF16), fg.reshape(1, d))


def _mem_kv_kernel(mem_ref, g_ref, wkv_ref, k_ref, v_ref, *, d):
    m = _rms(mem_ref[0], g_ref[...]).astype(BF16)
    kv = _dot(m, wkv_ref[...])
    k_ref[0] = kv[:, :d].astype(BF16)
    v_ref[0] = kv[:, d:].astype(BF16)


def _xattn_kernel(x_ref, g_ref, wq_ref, k_ref, v_ref, wo_ref, o_ref, *, heads):
    x = x_ref[0]
    d = x.shape[-1]
    hd = d // heads
    h = _rms(x, g_ref[...]).astype(BF16)
    q = (_dot(h, wq_ref[...]) * (hd ** -0.5)).astype(BF16)
    outs = []
    for i in range(heads):
        s = _dot_nt(q[:, i * hd:(i + 1) * hd], k_ref[0, :, i * hd:(i + 1) * hd])
        m = jnp.max(s, axis=-1, keepdims=True)
        p = jnp.exp(s - m)
        l = jnp.sum(p, axis=-1, keepdims=True)
        o = _dot(p.astype(BF16), v_ref[0, :, i * hd:(i + 1) * hd])
        outs.append((o / l).astype(BF16))
    o = jnp.concatenate(outs, axis=-1)
    o_ref[0] = x + _dot(o, wo_ref[...])


def _xattn(x, mem, g_x, g_mem, w_q, w_kv, w_o, tm=512):
    b, t, d = x.shape
    ml = mem.shape[1]
    k, v = pl.pallas_call(
        functools.partial(_mem_kv_kernel, d=d),
        grid=(b,),
        in_specs=[
            pl.BlockSpec((1, ml, d), lambda i: (i, 0, 0)),
            _const_spec((1, d)),
            _const_spec((d, 2 * d)),
        ],
        out_specs=[pl.BlockSpec((1, ml, d), lambda i: (i, 0, 0))] * 2,
        out_shape=[jax.ShapeDtypeStruct((b, ml, d), BF16)] * 2,
        compiler_params=_cparams("parallel"),
        name="mem_kv",
    )(mem, g_mem.reshape(1, d), w_kv.astype(BF16))
    tm = min(tm, t)
    return pl.pallas_call(
        functools.partial(_xattn_kernel, heads=X_HEADS),
        grid=(b, t // tm),
        in_specs=[
            pl.BlockSpec((1, tm, d), lambda i, j: (i, j, 0)),
            _const_spec((1, d)),
            _const_spec((d, d)),
            pl.BlockSpec((1, ml, d), lambda i, j: (i, 0, 0)),
            pl.BlockSpec((1, ml, d), lambda i, j: (i, 0, 0)),
            _const_spec((d, d)),
        ],
        out_specs=pl.BlockSpec((1, tm, d), lambda i, j: (i, j, 0)),
        out_shape=jax.ShapeDtypeStruct((b, t, d), F32),
        compiler_params=_cparams("parallel", "parallel"),
        name="xattn",
    )(x, g_x.reshape(1, d), w_q.astype(BF16), k, v, w_o.astype(BF16))


def _fold_nt_kernel(a_ref, b_ref, o_ref, *, scale):
    o = lax.dot_general(a_ref[0], b_ref[0], (((1,), (1,)), ((), ())),
                        preferred_element_type=F32, precision=lax.Precision.HIGHEST)
    o_ref[...] = (o * scale).astype(o_ref.dtype)


def _fold_nn_kernel(a_ref, b_ref, o_ref):
    o = jnp.dot(a_ref[0], b_ref[0], preferred_element_type=F32, precision=lax.Precision.HIGHEST)
    o_ref[0] = o.astype(o_ref.dtype)


def _fold_weights(w_uq, w_uk, w_uv, w_o, scale):
    nh = A_HEADS
    uq = w_uq.reshape(A_LORA, nh, A_NOPE + A_ROPE)
    uq_nope = jnp.transpose(uq[:, :, :A_NOPE], (1, 0, 2))
    w_qlat = pl.pallas_call(
        functools.partial(_fold_nt_kernel, scale=scale),
        grid=(nh,),
        in_specs=[pl.BlockSpec((1, A_LORA, A_NOPE), lambda h: (h, 0, 0)),
                  pl.BlockSpec((1, A_LORA, A_NOPE), lambda h: (h, 0, 0))],
        out_specs=pl.BlockSpec((A_LORA, A_LORA), lambda h: (0, h)),
        out_shape=jax.ShapeDtypeStruct((A_LORA, nh * A_LORA), BF16),
        compiler_params=_cparams("parallel"),
        name="fold_qlat",
    )(uq_nope, w_uk)
    d = w_o.shape[1]
    w_ov = pl.pallas_call(
        _fold_nn_kernel,
        grid=(nh,),
        in_specs=[pl.BlockSpec((1, A_LORA, A_V), lambda h: (h, 0, 0)),
                  pl.BlockSpec((1, A_V, d), lambda h: (h, 0, 0))],
        out_specs=pl.BlockSpec((1, A_LORA, d), lambda h: (h, 0, 0)),
        out_shape=jax.ShapeDtypeStruct((nh, A_LORA, d), BF16),
        compiler_params=_cparams("parallel"),
        name="fold_ov",
    )(w_uv, w_o.reshape(nh, A_V, d))
    return w_qlat, w_ov


def _dsa_proj_kernel(x_ref, pos_ref, g_ref, wa_ref, nq_ref, nkv_ref, lng_ref, lnb_ref, inv64_ref, inv32_ref,
                     wqi_ref, wpe_ref, wql_ref,
                     kv_ref, kvt_ref, kidx_ref, qidx_ref, widx_ref, q_ref):
    x = x_ref[0]
    tm = x.shape[0]
    h = _rms(x, g_ref[...]).astype(BF16)
    p = _dot(h, wa_ref[...])
    cq = _rms(p[:, 0:256], nq_ref[...])
    ckv = _rms(p[:, 256:512], nkv_ref[...])
    posf = pos_ref[0].astype(F32)
    a64 = posf * inv64_ref[...]
    c64, s64 = jnp.cos(a64), jnp.sin(a64)
    a32 = posf * inv32_ref[...]
    c32, s32 = jnp.cos(a32), jnp.sin(a32)
    lane = lax.broadcasted_iota(I32, (tm, 128), 1)

    g2 = p[:, 512:640]
    kr = g2 * c32 + pltpu.roll(g2, 96, 1) * s32
    kr = jnp.where(lane < A_ROPE, kr, 0.0)
    kv_ref[0, :, 0:256] = ckv.astype(BF16)
    kvt_ref[0, 0] = ckv.T.astype(BF16)
    kv_ref[0, :, 256:384] = kr.astype(BF16)

    g3 = p[:, 640:768]
    valid = lane < IDX_DIM
    mu = jnp.sum(jnp.where(valid, g3, 0.0), axis=-1, keepdims=True) * (1.0 / IDX_DIM)
    dlt = g3 - mu
    var = jnp.sum(jnp.where(valid, dlt * dlt, 0.0), axis=-1, keepdims=True) * (1.0 / IDX_DIM)
    y = dlt * lax.rsqrt(var + RMS_EPS) * lng_ref[...] + lnb_ref[...]
    ki = y * c64 + pltpu.roll(y, 64, 1) * s64
    ki = jnp.where(valid, ki, pltpu.roll(ki, 64, 1))
    ki = ki.astype(BF16)
    kidx_ref[0] = jnp.concatenate([ki, ki], axis=-1)

    g4 = p[:, 768:896] * (IDX_HEADS ** -0.5)
    widx_ref[0] = g4.T[0:8, :]

    cqb = cq.astype(BF16)
    c64x2 = jnp.concatenate([c64, c64], axis=-1)
    s64x2 = jnp.concatenate([s64, s64], axis=-1)
    qi = _dot(cqb, wqi_ref[:, 0:256]) * c64x2 + _dot(cqb, wqi_ref[:, 256:512]) * s64x2
    qidx_ref[0] = qi.astype(BF16)

    c32x4 = jnp.concatenate([c32] * 4, axis=-1)
    s32x4 = jnp.concatenate([s32] * 4, axis=-1)
    qpe = _dot(cqb, wpe_ref[:, 0:512]) * c32x4 + _dot(cqb, wpe_ref[:, 512:1024]) * s32x4
    for hh in range(A_HEADS):
        ql = _dot(cqb, wql_ref[:, hh * 256:(hh + 1) * 256])
        q_ref[0, hh, :, 0:256] = ql.astype(BF16)
        tile = qpe[:, (hh // 4) * 128:(hh // 4 + 1) * 128]
        off = (hh % 4) * A_ROPE
        if off:
            tile = pltpu.roll(tile, 128 - off, 1)
        q_ref[0, hh, :, 256:384] = jnp.where(lane < A_ROPE, tile, 0.0).astype(BF16)


def _rot_half_cols(w, width):
    k = w.shape[0]
    wg = w.reshape(k, -1, width)
    half = A_ROPE // 2
    sw = jnp.concatenate([-wg[:, :, half:A_ROPE], wg[:, :, :half],
                          jnp.zeros((k, wg.shape[1], width - A_ROPE), w.dtype)], axis=-1)
    return sw.reshape(k, -1)


def _dsa_proj(x, positions, g_mix, w_in, norm_q, norm_kv, kidx_g, kidx_b, w_uq, w_qidx, w_qlat, scale, tm=512):
    b, t, d = x.shape
    tm = min(tm, t)
    half = A_ROPE // 2
    o_kr = 2 * A_LORA
    o_ki = o_kr + A_ROPE
    o_w = o_ki + IDX_DIM
    zeros = lambda n: jnp.zeros((d, n), F32)
    w_kr = w_in[:, o_kr:o_ki]
    w_ki = w_in[:, o_ki:o_w]
    w_ki_perm = jnp.concatenate([w_ki[:, half:A_ROPE], w_ki[:, :half]], axis=-1)
    wa = jnp.concatenate([
        w_in[:, :o_kr],
        w_kr, _rot_half_cols(w_kr, A_ROPE), zeros(64),
        w_ki, w_ki_perm, zeros(32),
        w_in[:, o_w:o_w + IDX_HEADS], zeros(128 - IDX_HEADS)], axis=-1).astype(BF16)
    sgn = jnp.concatenate([-jnp.ones((half,), F32), jnp.ones((half,), F32)])
    perm = lambda v: jnp.concatenate([v[half:A_ROPE], v[:half]])
    lng = jnp.concatenate([kidx_g, sgn * perm(kidx_g), jnp.zeros((32,), F32)]).reshape(1, 128)
    lnb = jnp.concatenate([kidx_b, sgn * perm(kidx_b), jnp.zeros((32,), F32)]).reshape(1, 128)
    inv = ROPE_THETA ** (-jnp.arange(0, A_ROPE, 2, dtype=F32) / A_ROPE)
    inv64 = jnp.tile(jnp.concatenate([inv, inv, jnp.zeros((32,), F32)]), 2).reshape(1, 128)
    inv32 = jnp.tile(inv, 8).reshape(1, 128)
    wqi = jnp.concatenate([w_qidx, _rot_half_cols(w_qidx, IDX_DIM)], axis=-1).astype(BF16)
    uq = w_uq.reshape(A_LORA, A_HEADS, A_NOPE + A_ROPE)
    w_pe = (uq[:, :, A_NOPE:] * scale).reshape(A_LORA, A_HEADS * A_ROPE)
    wpe = jnp.concatenate([w_pe, _rot_half_cols(w_pe, A_ROPE)], axis=-1).astype(BF16)
    row = lambda a: a.reshape(1, -1)
    outs = pl.pallas_call(
        _dsa_proj_kernel,
        grid=(b, t // tm),
        in_specs=[
            pl.BlockSpec((1, tm, d), lambda i, j: (i, j, 0)),
            pl.BlockSpec((1, tm, 1), lambda i, j: (i, j, 0)),
            _const_spec((1, d)),
            _const_spec(wa.shape),
            _const_spec((1, A_LORA)), _const_spec((1, A_LORA)),
            _const_spec((1, 128)), _const_spec((1, 128)), _const_spec((1, 128)), _const_spec((1, 128)),
            _const_spec(wqi.shape), _const_spec(wpe.shape), _const_spec(w_qlat.shape),
        ],
        out_specs=[
            pl.BlockSpec((1, tm, KV_W), lambda i, j: (i, j, 0)),
            pl.BlockSpec((1, 1, A_LORA, tm), lambda i, j: (i, j, 0, 0)),
            pl.BlockSpec((1, tm, 256), lambda i, j: (i, j, 0)),
            pl.BlockSpec((1, tm, 256), lambda i, j: (i, j, 0)),
            pl.BlockSpec((1, 8, tm), lambda i, j: (i, 0, j)),
            pl.BlockSpec((1, A_HEADS, tm, KV_W), lambda i, j: (i, 0, j, 0)),
        ],
        out_shape=[
            jax.ShapeDtypeStruct((b, t, KV_W), BF16),
            jax.ShapeDtypeStruct((b, t // tm, A_LORA, tm), BF16),
            jax.ShapeDtypeStruct((b, t, 256), BF16),
            jax.ShapeDtypeStruct((b, t, 256), BF16),
            jax.ShapeDtypeStruct((b, 8, t), F32),
            jax.ShapeDtypeStruct((b, A_HEADS, t, KV_W), BF16),
        ],
        compiler_params=_cparams("parallel", "parallel"),
        name="dsa_proj",
    )(x, positions.reshape(b, t, 1), row(g_mix), wa, row(norm_q), row(norm_kv), lng, lnb, inv64, inv32,
      wqi, wpe, w_qlat)
    return outs


def _dsa_attn_kernel(x_ref, q_ref, qidx_ref, widx_ref, kidx_ref, kv_ref, kvt_ref, wov_ref, ltri_ref,
                     o_ref, keys_ref, khi_ref, klo_ref, m_ref, l_ref, acc_ref, bias_ref, tie_ref, *, tq, kb, topk):
    i = pl.program_id(1)
    t0 = i * tq
    nkb = (t0 + tq + kb - 1) // kb
    nh = A_HEADS
    rows = nh * tq

    lane_q = lax.broadcasted_iota(I32, (1, tq), 1)
    limit = ((t0 + lane_q) // CHUNK + 1) * CHUNK
    row_k = lax.broadcasted_iota(I32, (kb, tq), 0)

    qi = qidx_ref[0]
    lane256 = lax.broadcasted_iota(I32, (tq, 256), 1)
    qcat = jnp.concatenate([jnp.where((lane256 // IDX_DIM) == hh, qi, jnp.zeros_like(qi))
                            for hh in range(IDX_HEADS)], axis=0)
    wrows = [widx_ref[0, hh:hh + 1, :] for hh in range(IDX_HEADS)]

    def score_block(j, carry):
        kblk = kidx_ref[0, pl.ds(pl.multiple_of(j * kb, kb), kb), :]
        logit = _dot_nt(kblk, qcat)
        sc = jnp.zeros((kb, tq), F32)
        for hh in range(IDX_HEADS):
            sc = sc + jnp.maximum(logit[:, hh * tq:(hh + 1) * tq], 0.0) * wrows[hh]
        sc = sc * (IDX_DIM ** -0.5)
        bits = lax.bitcast_convert_type(sc, I32)
        bits = jnp.where(bits == INT_MIN, 0, bits)
        key = bits ^ ((bits >> 31) & 0x7FFFFFFF)
        adm = (j * kb + row_k) < limit
        key = jnp.where(adm, key, KEY_NEG_INF)
        keys_ref[j] = key
        khi_ref[j] = (key >> 16).astype(I16)
        klo_ref[j] = ((key & 0xFFFF) - 32768).astype(I16)
        return carry

    lax.fori_loop(0, nkb, score_block, 0)

    def count(pred_fn):
        def body(j, c):
            hit = jnp.where(pred_fn(keys_ref[j]), 1, 0).astype(I32)
            return c + jnp.sum(hit.reshape(kb // 8, 8, tq), axis=0)
        c = lax.fori_loop(0, nkb, body, jnp.zeros((8, tq), I32))
        return jnp.sum(c, axis=0, keepdims=True)

    one16 = jnp.ones((16, tq), I16)
    zero16 = jnp.zeros((16, tq), I16)

    def count16(pred_fn):
        def body(j, c):
            hit = pred_fn(khi_ref[j])
            parts = [jnp.where(hit[r * 16:(r + 1) * 16], one16, zero16) for r in range(kb // 16)]
            while len(parts) > 1:
                parts = [a + b for a, b in zip(parts[0::2], parts[1::2])]
            return c + parts[0]
        c = lax.fori_loop(0, nkb, body, zero16)
        return jnp.sum(c.astype(I32), axis=0, keepdims=True)

    def radix16(target):
        def step(bi, t16):
            cand = t16 + jnp.left_shift(jnp.int32(1), 15 - bi)
            cand16 = cand.astype(I16)
            cnt = count16(lambda k: k >= cand16)
            return jnp.where(cnt >= target, cand, t16)
        return lax.fori_loop(0, 16, step, jnp.full((1, tq), -32768, I32))

    t_hi = radix16(topk)
    t_hi16 = t_hi.astype(I16)
    need_lo = topk - count16(lambda k: k > t_hi16)

    def mask_lo(j, carry):
        khi_ref[j] = jnp.where(khi_ref[j] == t_hi16, klo_ref[j], jnp.full((kb, tq), -32768, I16))
        return carry

    lax.fori_loop(0, nkb, mask_lo, 0)
    t_lo = radix16(need_lo)
    thr = (t_hi << 16) | (t_lo + 32768)
    need_i = topk - count(lambda k: k > thr)
    need = need_i.astype(F32)
    tie_break = jnp.max(jnp.where(count(lambda k: k == thr) > need_i, 1, 0)) > 0

    m_ref[...] = jnp.full(m_ref.shape, -jnp.inf, F32)
    l_ref[...] = jnp.zeros(l_ref.shape, F32)
    acc_ref[...] = jnp.zeros(acc_ref.shape, F32)
    npair = nh // 2

    tie_ref[...] = jnp.zeros(tie_ref.shape, F32)

    def attn_block(j, carry):
        key = keys_ref[j]
        adm = (j * kb + row_k) < limit

        @pl.when(jnp.logical_not(tie_break))
        def _():
            bias_ref[...] = jnp.where(adm & (key >= thr), 0.0, NEG_BIG)

        @pl.when(tie_break)
        def _():
            eq = key == thr
            pref = _dot(ltri_ref[...], jnp.where(eq, 1.0, 0.0).astype(BF16))
            rank = tie_ref[0:1, :] + pref
            sel = adm & ((key > thr) | (eq & (rank <= need)))
            bias_ref[...] = jnp.where(sel, 0.0, NEG_BIG)
            tie_ref[0:1, :] = tie_ref[0:1, :] + pref[kb - 1:kb, :]

        bias = bias_ref[...]
        bias2 = jnp.concatenate([bias, bias], axis=-1)
        kvb = kv_ref[0, pl.ds(pl.multiple_of(j * kb, kb), kb), :]
        kvt = kvt_ref[0, j]
        hk = kb // 2
        hl = A_LORA // 2

        def logits(hp):
            qp = q_ref[0, 2 * hp:2 * hp + 2].reshape(2 * tq, KV_W)
            return jnp.concatenate([_dot_nt(kvb[0:hk], qp), _dot_nt(kvb[hk:kb], qp)], axis=0)

        st_next = logits(0)
        for hp in range(npair):
            st = st_next + bias2
            if hp + 1 < npair:
                st_next = logits(hp + 1)
            m_prev = m_ref[hp:hp + 1, :]
            m_new = jnp.maximum(m_prev, jnp.max(st, axis=0, keepdims=True))
            alpha = jnp.exp2(m_prev - m_new)
            p = jnp.exp2(st - m_new)
            l_ref[hp:hp + 1, :] = alpha * l_ref[hp:hp + 1, :] + jnp.sum(p, axis=0, keepdims=True)
            m_ref[hp:hp + 1, :] = m_new
            pb = p.astype(BF16)
            acc_ref[hp, 0:hl] = acc_ref[hp, 0:hl] * alpha + _dot(kvt[0:hl], pb)
            acc_ref[hp, hl:] = acc_ref[hp, hl:] * alpha + _dot(kvt[hl:], pb)
        return carry

    lax.fori_loop(0, nkb, attn_block, 0)

    y = x_ref[0]
    for hp in range(npair):
        o_pair = (acc_ref[hp] / l_ref[hp:hp + 1, :]).T
        for e in range(2):
            y = y + _dot(o_pair[e * tq:(e + 1) * tq].astype(BF16), wov_ref[2 * hp + e])
    o_ref[0] = y


def _dsa_attn(x, q, qidx, widx, kidx, kv, kvt, w_ov, tq=128):
    b, t, d = x.shape
    kb = kvt.shape[-1]
    topk = min(TOPK_MAX, t // 4)
    tq = min(tq, t)
    assert t % kb == 0 and t % tq == 0 and tq % CHUNK == 0
    ltri = jnp.asarray(np.tril(np.ones((kb, kb), np.float32)), BF16)
    kern = functools.partial(_dsa_attn_kernel, tq=tq, kb=kb, topk=topk)
    per_batch = lambda shape: pl.BlockSpec(shape, lambda i, j: (i,) + (0,) * (len(shape) - 1),
                                           pipeline_mode=pl.Buffered(1))
    return pl.pallas_call(
        kern,
        grid=(b, t // tq),
        in_specs=[
            pl.BlockSpec((1, tq, d), lambda i, j: (i, j, 0)),
            pl.BlockSpec((1, A_HEADS, tq, KV_W), lambda i, j: (i, 0, j, 0)),
            pl.BlockSpec((1, tq, 256), lambda i, j: (i, j, 0)),
            pl.BlockSpec((1, 8, tq), lambda i, j: (i, 0, j)),
            per_batch((1, t, 256)),
            per_batch((1, t, KV_W)),
            per_batch((1, t // kb, A_LORA, kb)),
            _const_spec(w_ov.shape),
            _const_spec((kb, kb)),
        ],
        out_specs=pl.BlockSpec((1, tq, d), lambda i, j: (i, j, 0)),
        out_shape=jax.ShapeDtypeStruct((b, t, d), F32),
        scratch_shapes=[
            pltpu.VMEM((t // kb, kb, tq), I32),
            pltpu.VMEM((t // kb, kb, tq), I16),
            pltpu.VMEM((t // kb, kb, tq), I16),
            pltpu.VMEM((A_HEADS // 2, 2 * tq), F32),
            pltpu.VMEM((A_HEADS // 2, 2 * tq), F32),
            pltpu.VMEM((A_HEADS // 2, A_LORA, 2 * tq), F32),
            pltpu.VMEM((kb, tq), F32),
            pltpu.VMEM((8, tq), F32),
        ],
        compiler_params=_cparams("parallel", "parallel"),
        name="dsa_attn",
    )(x, q, qidx, widx, kidx, kv, kvt, w_ov, ltri)


def _dsa_layer(x, positions, g_mix, w_in, norm_q, norm_kv, kidx_g, kidx_b, w_uq, w_uk, w_uv, w_qidx, w_o):
    scale = (A_NOPE + A_ROPE) ** -0.5 * math.log2(math.e)
    w_qlat, w_ov = _fold_weights(w_uq, w_uk, w_uv, w_o, scale)
    kv, kvt, kidx, qidx, widx, q = _dsa_proj(x, positions, g_mix, w_in, norm_q, norm_kv, kidx_g, kidx_b,
                                        w_uq, w_qidx, w_qlat, scale)
    return _dsa_attn(x, q, qidx, widx, kidx, kv, kvt, w_ov)


def _gdn_proj_kernel(x_ref, g_ref, wqkv_ref, wz_ref, wba_ref, conv_ref, alog_ref, dtb_ref, lblk_ref,
                     q_ref, k_ref, v_ref, z_ref, bg_ref, buf_ref, *, tm, cw):
    @pl.when(pl.program_id(1) == 0)
    def _():
        buf_ref[0:8, :] = jnp.zeros((8, buf_ref.shape[1]), F32)

    x = x_ref[0]
    h = _rms(x, g_ref[...]).astype(BF16)
    qk_w = B_QK_HEADS * B_HEAD
    ncol = wqkv_ref.shape[1]
    for c in range(ncol // cw):
        lo = c * cw
        cur = _dot(h, wqkv_ref[:, lo:lo + cw])
        buf_ref[8:8 + tm, lo:lo + cw] = cur
        y = cur * conv_ref[3:4, lo:lo + cw]
        for tap in range(B_CONV - 1):
            sh = B_CONV - 1 - tap
            y = y + buf_ref[8 - sh:8 - sh + tm, lo:lo + cw] * conv_ref[tap:tap + 1, lo:lo + cw]
        buf_ref[0:8, lo:lo + cw] = buf_ref[tm:tm + 8, lo:lo + cw]
        y = _silu(y)
        for s in range(cw // B_HEAD):
            col = lo + s * B_HEAD
            yh = y[:, s * B_HEAD:(s + 1) * B_HEAD]
            if col < 2 * qk_w:
                yh = yh * lax.rsqrt(jnp.sum(yh * yh, axis=-1, keepdims=True) + RMS_EPS)
                if col < qk_w:
                    q_ref[0, :, col:col + B_HEAD] = (yh * (B_HEAD ** -0.5)).astype(BF16)
                else:
                    k_ref[0, :, col - qk_w:col - qk_w + B_HEAD] = yh.astype(BF16)
            else:
                v_ref[0, :, col - 2 * qk_w:col - 2 * qk_w + B_HEAD] = yh.astype(BF16)
    z_ref[0] = _dot(h, wz_ref[...]).astype(BF16)
    ba = _dot(h, wba_ref[...])
    beta = jax.nn.sigmoid(ba)
    sp_in = ba + dtb_ref[...]
    softplus = jnp.maximum(sp_in, 0.0) + jnp.log1p(jnp.exp(-jnp.abs(sp_in)))
    g = -jnp.exp(alog_ref[...]) * softplus
    gcum = jnp.dot(lblk_ref[...], g, preferred_element_type=F32, precision=lax.Precision.HIGHEST)
    lane = lax.broadcasted_iota(I32, (tm, 128), 1)
    bg_ref[0] = jnp.where(lane < B_V_HEADS, beta, gcum)


def _gdn_proj(x, g_mix, w_in, conv_w, a_log, dt_bias, tm=256):
    b, t, d = x.shape
    tm = min(tm, t)
    qk_w = B_QK_HEADS * B_HEAD
    v_w = B_V_HEADS * B_HEAD
    cwid = 2 * qk_w + v_w
    wqkv = w_in[:, :cwid].astype(BF16)
    wz = w_in[:, cwid:cwid + v_w].astype(BF16)
    wba = jnp.concatenate([w_in[:, cwid + v_w:], jnp.zeros((d, 128 - 2 * B_V_HEADS), F32)], axis=-1).astype(BF16)
    pad = lambda v: jnp.concatenate([jnp.zeros((B_V_HEADS,), F32), v, jnp.zeros((128 - 2 * B_V_HEADS,), F32)]).reshape(1, 128)
    blk = np.arange(tm) // CHUNK
    lblk = jnp.asarray(((blk[:, None] == blk[None, :]) & (np.arange(tm)[:, None] >= np.arange(tm)[None, :])).astype(np.float32))
    kern = functools.partial(_gdn_proj_kernel, tm=tm, cw=512)
    tok = lambda w: pl.BlockSpec((1, tm, w), lambda i, j: (i, j, 0))
    return pl.pallas_call(
        kern,
        grid=(b, t // tm),
        in_specs=[
            tok(d), _const_spec((1, d)), _const_spec(wqkv.shape), _const_spec(wz.shape), _const_spec(wba.shape),
            _const_spec((B_CONV, cwid)), _const_spec((1, 128)), _const_spec((1, 128)), _const_spec((tm, tm)),
        ],
        out_specs=[tok(qk_w), tok(qk_w), tok(v_w), tok(v_w), tok(128)],
        out_shape=[
            jax.ShapeDtypeStruct((b, t, qk_w), BF16),
            jax.ShapeDtypeStruct((b, t, qk_w), BF16),
            jax.ShapeDtypeStruct((b, t, v_w), BF16),
            jax.ShapeDtypeStruct((b, t, v_w), BF16),
            jax.ShapeDtypeStruct((b, t, 128), F32),
        ],
        scratch_shapes=[pltpu.VMEM((tm + 8, cwid), F32)],
        compiler_params=_cparams("parallel", "arbitrary"),
        name="gdn_proj",
    )(x, g_mix.reshape(1, d), wqkv, wz, wba, conv_w, pad(a_log), pad(dt_bias), lblk)


def _gdn_prep_kernel(q_ref, k_ref, v_ref, bg_ref, u_ref, w_ref, qg_ref, kd_ref, aqk_ref, eg_ref, *, tp):
    hv = pl.program_id(2)
    c = CHUNK
    ri = lax.broadcasted_iota(I32, (c, c), 0)
    ci = lax.broadcasted_iota(I32, (c, c), 1)
    eye = ri == ci
    lane = lax.broadcasted_iota(I32, (c, 128), 1)
    rowi = lax.broadcasted_iota(I32, (c, 1), 0)
    sls = [slice(n * c, (n + 1) * c) for n in range(tp // c)]
    bcols, gcols, decays, invs, xps = [], [], [], [], []
    for n, sl in enumerate(sls):
        bg = bg_ref[0, sl, :]
        bcol = jnp.sum(jnp.where(lane == hv, bg, 0.0), axis=-1, keepdims=True)
        gcol = jnp.sum(jnp.where(lane == hv + B_V_HEADS, bg, 0.0), axis=-1, keepdims=True)
        grow = jnp.sum(jnp.where(eye, gcol, 0.0), axis=0, keepdims=True)
        glast = jnp.sum(jnp.where(rowi == c - 1, gcol, 0.0), axis=0, keepdims=True)
        kb_ = k_ref[0, sl, :]
        qb_ = q_ref[0, sl, :]
        decay = jnp.exp(jnp.where(ri >= ci, gcol - grow, -jnp.inf))
        x_ = jnp.where(ri > ci, -(bcol * _dot_nt(kb_, kb_) * decay), 0.0)
        qg_ref[0, sl, :] = (qb_.astype(F32) * jnp.exp(gcol)).astype(BF16)
        kd_ref[0, sl, :] = (kb_.astype(F32) * jnp.exp(glast - gcol)).astype(BF16)
        aqk = jnp.where(ri >= ci, _dot_nt(qb_, kb_) * decay, 0.0)
        aqk_ref[0, sl, :] = jnp.concatenate([aqk, jnp.zeros_like(aqk)], axis=-1).astype(BF16)
        eg_ref[0, 0, n:n + 1, :] = jnp.broadcast_to(jnp.exp(glast), (1, 128))
        bcols.append(bcol)
        gcols.append(gcol)
        xps.append(x_)
        invs.append(jnp.where(eye, 1.0, 0.0) + x_)
    for _ in range(5):
        xbs = [xp.astype(BF16) for xp in xps]
        xps = [_dot(xb, xb) for xb in xbs]
        invs = [inv + _dot(inv.astype(BF16), xp.astype(BF16)) for inv, xp in zip(invs, xps)]
    for sl, inv, bcol, gcol in zip(sls, invs, bcols, gcols):
        rhs = jnp.concatenate([v_ref[0, sl, :].astype(F32) * bcol,
                               k_ref[0, sl, :].astype(F32) * (bcol * jnp.exp(gcol))], axis=-1)
        uw = _dot(inv.astype(BF16), rhs.astype(BF16))
        u_ref[0, sl, :] = uw[:, :B_HEAD]
        w_ref[0, sl, :] = uw[:, B_HEAD:].astype(BF16)


def _gdn_prep(q, k, v, bg, tp=1024):
    b, t, _ = q.shape
    tp = min(tp, t)
    nv = B_V_HEADS
    rep = B_V_HEADS // B_QK_HEADS
    v_w = nv * B_HEAD
    hspec = lambda: pl.BlockSpec((1, tp, B_HEAD), lambda i, j, h: (i, j, h))
    qkspec = lambda: pl.BlockSpec((1, tp, B_HEAD), lambda i, j, h: (i, j, h // rep))
    return pl.pallas_call(
        functools.partial(_gdn_prep_kernel, tp=tp),
        grid=(b, t // tp, nv),
        in_specs=[qkspec(), qkspec(), hspec(), pl.BlockSpec((1, tp, 128), lambda i, j, h: (i, j, 0))],
        out_specs=[hspec(), hspec(), hspec(), hspec(), hspec(),
                   pl.BlockSpec((1, 1, tp // CHUNK, 128), lambda i, j, h: (i, h, j, 0))],
        out_shape=[
            jax.ShapeDtypeStruct((b, t, v_w), F32),
            jax.ShapeDtypeStruct((b, t, v_w), BF16),
            jax.ShapeDtypeStruct((b, t, v_w), BF16),
            jax.ShapeDtypeStruct((b, t, v_w), BF16),
            jax.ShapeDtypeStruct((b, t, v_w), BF16),
            jax.ShapeDtypeStruct((b, nv, t // CHUNK, 128), F32),
        ],
        compiler_params=_cparams("parallel", "parallel", "parallel"),
        name="gdn_prep",
    )(q, k, v, bg)


def _gdn_scan_kernel(u_ref, w_ref, qg_ref, kd_ref, aqk_ref, eg_ref, z_ref, no_ref, o_ref, state_ref, *, tt, hb):
    @pl.when(pl.program_id(2) == 0)
    def _():
        state_ref[...] = jnp.zeros(state_ref.shape, F32)

    c = CHUNK

    def chunk_body(n, carry):
        r0 = pl.multiple_of(n * c, c)
        rs = pl.ds(r0, c)
        lss = [slice(hh * B_HEAD, (hh + 1) * B_HEAD) for hh in range(hb)]
        sts = [state_ref[hh] for hh in range(hb)]
        stbs = [st.astype(BF16) for st in sts]
        vbs = [(u_ref[0, rs, ls] - _dot(w_ref[0, rs, ls], stb)).astype(BF16) for ls, stb in zip(lss, stbs)]
        os_ = [_dot(qg_ref[0, rs, ls], stb) + _dot(aqk_ref[0, rs, ls][:, 0:c], vb)
               for ls, stb, vb in zip(lss, stbs, vbs)]
        for hh, (ls, st, vb) in enumerate(zip(lss, sts, vbs)):
            eg = eg_ref[0, hh, pl.ds(n, 1), :]
            upd = lax.dot_general(kd_ref[0, rs, ls], vb, (((0,), (0,)), ((), ())), preferred_element_type=F32)
            state_ref[hh] = st * eg + upd
        for ls, o in zip(lss, os_):
            on = o * lax.rsqrt(jnp.mean(o * o, axis=-1, keepdims=True) + RMS_EPS) * no_ref[...]
            o_ref[0, rs, ls] = (on * _silu(z_ref[0, rs, ls].astype(F32))).astype(BF16)
        return carry

    lax.fori_loop(0, tt // c, chunk_body, 0)


def _gdn_scan(u, w, qg, kd, aqk, eg, z, norm_o, tt=1024, hb=8):
    b, t, v_w = u.shape
    tt = min(tt, t)
    nv = B_V_HEADS
    wid = hb * B_HEAD
    spec = lambda: pl.BlockSpec((1, tt, wid), lambda i, g, j: (i, j, g))
    return pl.pallas_call(
        functools.partial(_gdn_scan_kernel, tt=tt, hb=hb),
        grid=(b, nv // hb, t // tt),
        in_specs=[spec(), spec(), spec(), spec(), spec(),
                  pl.BlockSpec((1, hb, tt // CHUNK, 128), lambda i, g, j: (i, g, j, 0)),
                  spec(), _const_spec((1, B_HEAD))],
        out_specs=spec(),
        out_shape=jax.ShapeDtypeStruct((b, t, v_w), BF16),
        scratch_shapes=[pltpu.VMEM((hb, B_HEAD, B_HEAD), F32)],
        compiler_params=_cparams("parallel", "parallel", "arbitrary"),
        name="gdn_scan",
    )(u, w, qg, kd, aqk, eg, z, norm_o.reshape(1, B_HEAD))


def _proj_res_kernel(x_ref, a_ref, w_ref, o_ref):
    o_ref[...] = x_ref[...] + _dot(a_ref[...], w_ref[...])


def _proj_res(x2, a2, w, tm=512):
    n, d = x2.shape
    kdim = a2.shape[1]
    tm = min(tm, n)
    return pl.pallas_call(
        _proj_res_kernel,
        grid=(n // tm,),
        in_specs=[pl.BlockSpec((tm, d), lambda i: (i, 0)), pl.BlockSpec((tm, kdim), lambda i: (i, 0)),
                  _const_spec((kdim, d))],
        out_specs=pl.BlockSpec((tm, d), lambda i: (i, 0)),
        out_shape=jax.ShapeDtypeStruct((n, d), F32),
        compiler_params=_cparams("parallel"),
        name="proj_res",
    )(x2, a2, w.astype(BF16))


def _gdn_layer(x, g_mix, w_in, conv_w, a_log, dt_bias, norm_o, w_o):
    b, t, d = x.shape
    q, k, v, z, bg = _gdn_proj(x, g_mix, w_in, conv_w, a_log, dt_bias)
    u, w, qg, kd, aqk, eg = _gdn_prep(q, k, v, bg)
    o = _gdn_scan(u, w, qg, kd, aqk, eg, z, norm_o)
    return _proj_res(x.reshape(b * t, d), o.reshape(b * t, -1), w_o).reshape(b, t, d)


def kernel(x, mem, positions, ffn1_norm, ffn1_w_in, ffn1_w_out, mix_norm, xattn_norm, mem_norm, xattn_w_q, xattn_w_kv, xattn_w_o, ffn2_norm, ffn2_w_in, ffn2_w_out, a_w_in, a_norm_q, a_norm_kv, a_kidx_g, a_kidx_b, a_w_uq, a_w_uk, a_w_uv, a_w_qidx, a_w_o, b_w_in, b_conv, b_a_log, b_dt_bias, b_norm_o, b_w_o, final_norm):
    b, t, d = x.shape
    depth = ffn1_norm.shape[0]
    n_mixers = 2
    for i in range(depth):
        x = _ffn(x.reshape(b * t, d), ffn1_norm[i], ffn1_w_in[i], ffn1_w_out[i]).reshape(b, t, d)
        j = i // n_mixers
        if i % n_mixers == 0:
            x = _dsa_layer(x, positions, mix_norm[i], a_w_in[j], a_norm_q[j], a_norm_kv[j], a_kidx_g[j],
                           a_kidx_b[j], a_w_uq[j], a_w_uk[j], a_w_uv[j], a_w_qidx[j], a_w_o[j])
        else:
            x = _gdn_layer(x, mix_norm[i], b_w_in[j], b_conv[j], b_a_log[j], b_dt_bias[j], b_norm_o[j], b_w_o[j])
        x = _xattn(x, mem, xattn_norm[i], mem_norm[i], xattn_w_q[i], xattn_w_kv[i], xattn_w_o[i])
        last = i == depth - 1
        x = _ffn(x.reshape(b * t, d), ffn2_norm[i], ffn2_w_in[i], ffn2_w_out[i],
                 final_g=final_norm if last else None).reshape(b, t, d)
    return x
```

```python
import functools
import math

import jax
import jax.numpy as jnp
import numpy as np
from jax import lax
from jax.experimental import pallas as pl
from jax.experimental.pallas import tpu as pltpu

F32 = jnp.float32
BF16 = jnp.bfloat16
I32 = jnp.int32
I16 = jnp.int16

RMS_EPS = 1e-6
ROPE_THETA = 10000.0
CHUNK = 64
TOPK_MAX = 256
A_HEADS = 16
A_LORA = 256
A_NOPE = 64
A_ROPE = 32
A_V = 64
IDX_HEADS = 4
IDX_DIM = 64
KV_W = 384
B_QK_HEADS = 8
B_V_HEADS = 16
B_HEAD = 128
B_CONV = 4
X_HEADS = 4
NEG_BIG = -1e30
INT_MIN = -2147483648
KEY_NEG_INF = -2139095041

VMEM_LIMIT_BYTES = 58 * 1024 * 1024


def _cparams(*sem):
    return pltpu.CompilerParams(dimension_semantics=sem, vmem_limit_bytes=VMEM_LIMIT_BYTES)


def _const_spec(shape):
    nd = len(shape)
    return pl.BlockSpec(shape, lambda *_: (0,) * nd, pipeline_mode=pl.Buffered(1))


def _rms(x, g):
    return x * lax.rsqrt(jnp.mean(x * x, axis=-1, keepdims=True) + RMS_EPS) * g


def _dot(a, b):
    return jnp.dot(a, b, preferred_element_type=F32)


def _dot_nt(a, b):
    return lax.dot_general(a, b, (((1,), (1,)), ((), ())), preferred_element_type=F32)


def _silu(x):
    return x * jax.nn.sigmoid(x)


def _ffn_kernel(x_ref, g_ref, win_ref, wout_ref, fg_ref, o_ref, acc_ref, *, d_ff, f_chunk, final_norm):
    x = x_ref[...]
    h = _rms(x, g_ref[...]).astype(BF16)
    for c in range(d_ff // f_chunk):
        lo = c * f_chunk
        gate = _dot(h, win_ref[:, lo:lo + f_chunk])
        up = _dot(h, win_ref[:, d_ff + lo:d_ff + lo + f_chunk])
        a = (_silu(gate) * up).astype(BF16)
        y = _dot(a, wout_ref[lo:lo + f_chunk, :])
        if c == 0:
            acc_ref[...] = y
        else:
            acc_ref[...] += y
    out = x + 0.5 * acc_ref[...]
    if final_norm:
        out = _rms(out, fg_ref[...])
    o_ref[...] = out


def _ffn(x2, g, w_in, w_out, final_g=None, tm=512):
    n, d = x2.shape
    d_ff = w_out.shape[0]
    f_chunk = 256
    assert n % tm == 0 and d_ff % f_chunk == 0
    fg = final_g if final_g is not None else g
    kern = functools.partial(_ffn_kernel, d_ff=d_ff, f_chunk=f_chunk, final_norm=final_g is not None)
    return pl.pallas_call(
        kern,
        grid=(n // tm,),
        in_specs=[
            pl.BlockSpec((tm, d), lambda i: (i, 0)),
            _const_spec((1, d)),
            _const_spec((d, 2 * d_ff)),
            _const_spec((d_ff, d)),
            _const_spec((1, d)),
        ],
        out_specs=pl.BlockSpec((tm, d), lambda i: (i, 0)),
        out_shape=jax.ShapeDtypeStruct((n, d), F32),
        scratch_shapes=[pltpu.VMEM((tm, d), F32)],
        compiler_params=_cparams("parallel"),
        name="ffn",
    )(x2, g.reshape(1, d), w_in.astype(BF16), w_out.astype(BF16), fg.reshape(1, d))


def _mem_kv_kernel(mem_ref, g_ref, wkv_ref, k_ref, v_ref, *, d):
    m = _rms(mem_ref[0], g_ref[...]).astype(BF16)
    kv = _dot(m, wkv_ref[...])
    k_ref[0] = kv[:, :d].astype(BF16)
    v_ref[0] = kv[:, d:].astype(BF16)


def _xattn_kernel(x_ref, g_ref, wq_ref, k_ref, v_ref, wo_ref, o_ref, *, heads):
    x = x_ref[0]
    d = x.shape[-1]
    hd = d // heads
    h = _rms(x, g_ref[...]).astype(BF16)
    q = (_dot(h, wq_ref[...]) * (hd ** -0.5)).astype(BF16)
    outs = []
    for i in range(heads):
        s = _dot_nt(q[:, i * hd:(i + 1) * hd], k_ref[0, :, i * hd:(i + 1) * hd])
        m = jnp.max(s, axis=-1, keepdims=True)
        p = jnp.exp(s - m)
        l = jnp.sum(p, axis=-1, keepdims=True)
        o = _dot(p.astype(BF16), v_ref[0, :, i * hd:(i + 1) * hd])
        outs.append((o / l).astype(BF16))
    o = jnp.concatenate(outs, axis=-1)
    o_ref[0] = x + _dot(o, wo_ref[...])


def _xattn(x, mem, g_x, g_mem, w_q, w_kv, w_o, tm=512):
    b, t, d = x.shape
    ml = mem.shape[1]
    k, v = pl.pallas_call(
        functools.partial(_mem_kv_kernel, d=d),
        grid=(b,),
        in_specs=[
            pl.BlockSpec((1, ml, d), lambda i: (i, 0, 0)),
            _const_spec((1, d)),
            _const_spec((d, 2 * d)),
        ],
        out_specs=[pl.BlockSpec((1, ml, d), lambda i: (i, 0, 0))] * 2,
        out_shape=[jax.ShapeDtypeStruct((b, ml, d), BF16)] * 2,
        compiler_params=_cparams("parallel"),
        name="mem_kv",
    )(mem, g_mem.reshape(1, d), w_kv.astype(BF16))
    tm = min(tm, t)
    return pl.pallas_call(
        functools.partial(_xattn_kernel, heads=X_HEADS),
        grid=(b, t // tm),
        in_specs=[
            pl.BlockSpec((1, tm, d), lambda i, j: (i, j, 0)),
            _const_spec((1, d)),
            _const_spec((d, d)),
            pl.BlockSpec((1, ml, d), lambda i, j: (i, 0, 0)),
            pl.BlockSpec((1, ml, d), lambda i, j: (i, 0, 0)),
            _const_spec((d, d)),
        ],
        out_specs=pl.BlockSpec((1, tm, d), lambda i, j: (i, j, 0)),
        out_shape=jax.ShapeDtypeStruct((b, t, d), F32),
        compiler_params=_cparams("parallel", "parallel"),
        name="xattn",
    )(x, g_x.reshape(1, d), w_q.astype(BF16), k, v, w_o.astype(BF16))


def _fold_nt_kernel(a_ref, b_ref, o_ref, *, scale):
    o = lax.dot_general(a_ref[0], b_ref[0], (((1,), (1,)), ((), ())),
                        preferred_element_type=F32, precision=lax.Precision.HIGHEST)
    o_ref[...] = (o * scale).astype(o_ref.dtype)


def _fold_nn_kernel(a_ref, b_ref, o_ref):
    o = jnp.dot(a_ref[0], b_ref[0], preferred_element_type=F32, precision=lax.Precision.HIGHEST)
    o_ref[0] = o.astype(o_ref.dtype)


def _fold_weights(w_uq, w_uk, w_uv, w_o, scale):
    nh = A_HEADS
    uq = w_uq.reshape(A_LORA, nh, A_NOPE + A_ROPE)
    uq_nope = jnp.transpose(uq[:, :, :A_NOPE], (1, 0, 2))
    w_qlat = pl.pallas_call(
        functools.partial(_fold_nt_kernel, scale=scale),
        grid=(nh,),
        in_specs=[pl.BlockSpec((1, A_LORA, A_NOPE), lambda h: (h, 0, 0)),
                  pl.BlockSpec((1, A_LORA, A_NOPE), lambda h: (h, 0, 0))],
        out_specs=pl.BlockSpec((A_LORA, A_LORA), lambda h: (0, h)),
        out_shape=jax.ShapeDtypeStruct((A_LORA, nh * A_LORA), BF16),
        compiler_params=_cparams("parallel"),
        name="fold_qlat",
    )(uq_nope, w_uk)
    d = w_o.shape[1]
    w_ov = pl.pallas_call(
        _fold_nn_kernel,
        grid=(nh,),
        in_specs=[pl.BlockSpec((1, A_LORA, A_V), lambda h: (h, 0, 0)),
                  pl.BlockSpec((1, A_V, d), lambda h: (h, 0, 0))],
        out_specs=pl.BlockSpec((1, A_LORA, d), lambda h: (h, 0, 0)),
        out_shape=jax.ShapeDtypeStruct((nh, A_LORA, d), BF16),
        compiler_params=_cparams("parallel"),
        name="fold_ov",
    )(w_uv, w_o.reshape(nh, A_V, d))
    return w_qlat, w_ov


def _dsa_proj_kernel(x_ref, pos_ref, g_ref, wa_ref, nq_ref, nkv_ref, lng_ref, lnb_ref, inv64_ref, inv32_ref,
                     wqi_ref, wpe_ref, wql_ref,
                     kv_ref, kvt_ref, kidx_ref, qidx_ref, widx_ref, q_ref):
    x = x_ref[0]
    tm = x.shape[0]
    h = _rms(x, g_ref[...]).astype(BF16)
    p = _dot(h, wa_ref[...])
    cq = _rms(p[:, 0:256], nq_ref[...])
    ckv = _rms(p[:, 256:512], nkv_ref[...])
    posf = pos_ref[0].astype(F32)
    a64 = posf * inv64_ref[...]
    c64, s64 = jnp.cos(a64), jnp.sin(a64)
    a32 = posf * inv32_ref[...]
    c32, s32 = jnp.cos(a32), jnp.sin(a32)
    lane = lax.broadcasted_iota(I32, (tm, 128), 1)

    g2 = p[:, 512:640]
    kr = g2 * c32 + pltpu.roll(g2, 96, 1) * s32
    kr = jnp.where(lane < A_ROPE, kr, 0.0)
    kv_ref[0, :, 0:256] = ckv.astype(BF16)
    kvt_ref[0, 0] = ckv.T.astype(BF16)
    kv_ref[0, :, 256:384] = kr.astype(BF16)

    g3 = p[:, 640:768]
    valid = lane < IDX_DIM
    mu = jnp.sum(jnp.where(valid, g3, 0.0), axis=-1, keepdims=True) * (1.0 / IDX_DIM)
    dlt = g3 - mu
    var = jnp.sum(jnp.where(valid, dlt * dlt, 0.0), axis=-1, keepdims=True) * (1.0 / IDX_DIM)
    y = dlt * lax.rsqrt(var + RMS_EPS) * lng_ref[...] + lnb_ref[...]
    ki = y * c64 + pltpu.roll(y, 64, 1) * s64
    ki = jnp.where(valid, ki, pltpu.roll(ki, 64, 1))
    ki = ki.astype(BF16)
    kidx_ref[0] = jnp.concatenate([ki, ki], axis=-1)

    g4 = p[:, 768:896] * (IDX_HEADS ** -0.5)
    widx_ref[0] = g4.T[0:8, :]

    cqb = cq.astype(BF16)
    c64x2 = jnp.concatenate([c64, c64], axis=-1)
    s64x2 = jnp.concatenate([s64, s64], axis=-1)
    qi = _dot(cqb, wqi_ref[:, 0:256]) * c64x2 + _dot(cqb, wqi_ref[:, 256:512]) * s64x2
    qidx_ref[0] = qi.astype(BF16)

    c32x4 = jnp.concatenate([c32] * 4, axis=-1)
    s32x4 = jnp.concatenate([s32] * 4, axis=-1)
    qpe = _dot(cqb, wpe_ref[:, 0:512]) * c32x4 + _dot(cqb, wpe_ref[:, 512:1024]) * s32x4
    for hh in range(A_HEADS):
        ql = _dot(cqb, wql_ref[:, hh * 256:(hh + 1) * 256])
        q_ref[0, hh, :, 0:256] = ql.astype(BF16)
        tile = qpe[:, (hh // 4) * 128:(hh // 4 + 1) * 128]
        off = (hh % 4) * A_ROPE
        if off:
            tile = pltpu.roll(tile, 128 - off, 1)
        q_ref[0, hh, :, 256:384] = jnp.where(lane < A_ROPE, tile, 0.0).astype(BF16)


def _rot_half_cols(w, width):
    k = w.shape[0]
    wg = w.reshape(k, -1, width)
    half = A_ROPE // 2
    sw = jnp.concatenate([-wg[:, :, half:A_ROPE], wg[:, :, :half],
                          jnp.zeros((k, wg.shape[1], width - A_ROPE), w.dtype)], axis=-1)
    return sw.reshape(k, -1)


def _dsa_proj(x, positions, g_mix, w_in, norm_q, norm_kv, kidx_g, kidx_b, w_uq, w_qidx, w_qlat, scale, tm=512):
    b, t, d = x.shape
    tm = min(tm, t)
    half = A_ROPE // 2
    o_kr = 2 * A_LORA
    o_ki = o_kr + A_ROPE
    o_w = o_ki + IDX_DIM
    zeros = lambda n: jnp.zeros((d, n), F32)
    w_kr = w_in[:, o_kr:o_ki]
    w_ki = w_in[:, o_ki:o_w]
    w_ki_perm = jnp.concatenate([w_ki[:, half:A_ROPE], w_ki[:, :half]], axis=-1)
    wa = jnp.concatenate([
        w_in[:, :o_kr],
        w_kr, _rot_half_cols(w_kr, A_ROPE), zeros(64),
        w_ki, w_ki_perm, zeros(32),
        w_in[:, o_w:o_w + IDX_HEADS], zeros(128 - IDX_HEADS)], axis=-1).astype(BF16)
    sgn = jnp.concatenate([-jnp.ones((half,), F32), jnp.ones((half,), F32)])
    perm = lambda v: jnp.concatenate([v[half:A_ROPE], v[:half]])
    lng = jnp.concatenate([kidx_g, sgn * perm(kidx_g), jnp.zeros((32,), F32)]).reshape(1, 128)
    lnb = jnp.concatenate([kidx_b, sgn * perm(kidx_b), jnp.zeros((32,), F32)]).reshape(1, 128)
    inv = ROPE_THETA ** (-jnp.arange(0, A_ROPE, 2, dtype=F32) / A_ROPE)
    inv64 = jnp.tile(jnp.concatenate([inv, inv, jnp.zeros((32,), F32)]), 2).reshape(1, 128)
    inv32 = jnp.tile(inv, 8).reshape(1, 128)
    wqi = jnp.concatenate([w_qidx, _rot_half_cols(w_qidx, IDX_DIM)], axis=-1).astype(BF16)
    uq = w_uq.reshape(A_LORA, A_HEADS, A_NOPE + A_ROPE)
    w_pe = (uq[:, :, A_NOPE:] * scale).reshape(A_LORA, A_HEADS * A_ROPE)
    wpe = jnp.concatenate([w_pe, _rot_half_cols(w_pe, A_ROPE)], axis=-1).astype(BF16)
    row = lambda a: a.reshape(1, -1)
    outs = pl.pallas_call(
        _dsa_proj_kernel,
        grid=(b, t // tm),
        in_specs=[
            pl.BlockSpec((1, tm, d), lambda i, j: (i, j, 0)),
            pl.BlockSpec((1, tm, 1), lambda i, j: (i, j, 0)),
            _const_spec((1, d)),
            _const_spec(wa.shape),
            _const_spec((1, A_LORA)), _const_spec((1, A_LORA)),
            _const_spec((1, 128)), _const_spec((1, 128)), _const_spec((1, 128)), _const_spec((1, 128)),
            _const_spec(wqi.shape), _const_spec(wpe.shape), _const_spec(w_qlat.shape),
        ],
        out_specs=[
            pl.BlockSpec((1, tm, KV_W), lambda i, j: (i, j, 0)),
            pl.BlockSpec((1, 1, A_LORA, tm), lambda i, j: (i, j, 0, 0)),
            pl.BlockSpec((1, tm, 256), lambda i, j: (i, j, 0)),
            pl.BlockSpec((1, tm, 256), lambda i, j: (i, j, 0)),
            pl.BlockSpec((1, 8, tm), lambda i, j: (i, 0, j)),
            pl.BlockSpec((1, A_HEADS, tm, KV_W), lambda i, j: (i, 0, j, 0)),
        ],
        out_shape=[
            jax.ShapeDtypeStruct((b, t, KV_W), BF16),
            jax.ShapeDtypeStruct((b, t // tm, A_LORA, tm), BF16),
            jax.ShapeDtypeStruct((b, t, 256), BF16),
            jax.ShapeDtypeStruct((b, t, 256), BF16),
            jax.ShapeDtypeStruct((b, 8, t), F32),
            jax.ShapeDtypeStruct((b, A_HEADS, t, KV_W), BF16),
        ],
        compiler_params=_cparams("parallel", "parallel"),
        name="dsa_proj",
    )(x, positions.reshape(b, t, 1), row(g_mix), wa, row(norm_q), row(norm_kv), lng, lnb, inv64, inv32,
      wqi, wpe, w_qlat)
    return outs


def _dsa_attn_kernel(x_ref, q_ref, qidx_ref, widx_ref, kidx_ref, kv_ref, kvt_ref, wov_ref, ltri_ref,
                     o_ref, keys_ref, khi_ref, klo_ref, m_ref, l_ref, acc_ref, *, tq, kb, topk):
    i = pl.program_id(1)
    t0 = i * tq
    nkb = (t0 + tq + kb - 1) // kb
    nh = A_HEADS
    rows = nh * tq

    lane_q = lax.broadcasted_iota(I32, (1, tq), 1)
    limit = ((t0 + lane_q) // CHUNK + 1) * CHUNK
    row_k = lax.broadcasted_iota(I32, (kb, tq), 0)

    qi = qidx_ref[0]
    lane256 = lax.broadcasted_iota(I32, (tq, 256), 1)
    qcat = jnp.concatenate([jnp.where((lane256 // IDX_DIM) == hh, qi, jnp.zeros_like(qi))
                            for hh in range(IDX_HEADS)], axis=0)
    wrows = [widx_ref[0, hh:hh + 1, :] for hh in range(IDX_HEADS)]

    def score_block(j, carry):
        kblk = kidx_ref[0, pl.ds(pl.multiple_of(j * kb, kb), kb), :]
        logit = _dot_nt(kblk, qcat)
        sc = jnp.zeros((kb, tq), F32)
        for hh in range(IDX_HEADS):
            sc = sc + jnp.maximum(logit[:, hh * tq:(hh + 1) * tq], 0.0) * wrows[hh]
        sc = sc * (IDX_DIM ** -0.5)
        bits = lax.bitcast_convert_type(sc, I32)
        bits = jnp.where(bits == INT_MIN, 0, bits)
        key = bits ^ ((bits >> 31) & 0x7FFFFFFF)
        adm = (j * kb + row_k) < limit
        key = jnp.where(adm, key, KEY_NEG_INF)
        keys_ref[j] = key
        khi_ref[j] = (key >> 16).astype(I16)
        klo_ref[j] = ((key & 0xFFFF) - 32768).astype(I16)
        return carry

    lax.fori_loop(0, nkb, score_block, 0)

    def count(pred_fn):
        def body(j, c):
            hit = jnp.where(pred_fn(keys_ref[j]), 1, 0).astype(I32)
            return c + jnp.sum(hit.reshape(kb // 8, 8, tq), axis=0)
        c = lax.fori_loop(0, nkb, body, jnp.zeros((8, tq), I32))
        return jnp.sum(c, axis=0, keepdims=True)

    one16 = jnp.ones((16, tq), I16)
    zero16 = jnp.zeros((16, tq), I16)

    def count16(pred_fn):
        def body(j, c):
            hit = pred_fn(khi_ref[j])
            parts = [jnp.where(hit[r * 16:(r + 1) * 16], one16, zero16) for r in range(kb // 16)]
            while len(parts) > 1:
                parts = [a + b for a, b in zip(parts[0::2], parts[1::2])]
            return c + parts[0]
        c = lax.fori_loop(0, nkb, body, zero16)
        return jnp.sum(c.astype(I32), axis=0, keepdims=True)

    def radix16(target):
        def step(bi, t16):
            cand = t16 + jnp.left_shift(jnp.int32(1), 15 - bi)
            cand16 = cand.astype(I16)
            cnt = count16(lambda k: k >= cand16)
            return jnp.where(cnt >= target, cand, t16)
        return lax.fori_loop(0, 16, step, jnp.full((1, tq), -32768, I32))

    t_hi = radix16(topk)
    t_hi16 = t_hi.astype(I16)
    need_lo = topk - count16(lambda k: k > t_hi16)

    def mask_lo(j, carry):
        khi_ref[j] = jnp.where(khi_ref[j] == t_hi16, klo_ref[j], jnp.full((kb, tq), -32768, I16))
        return carry

    lax.fori_loop(0, nkb, mask_lo, 0)
    t_lo = radix16(need_lo)
    thr = (t_hi << 16) | (t_lo + 32768)
    need = (topk - count(lambda k: k > thr)).astype(F32)

    m_ref[...] = jnp.full(m_ref.shape, -jnp.inf, F32)
    l_ref[...] = jnp.zeros(l_ref.shape, F32)
    acc_ref[...] = jnp.zeros(acc_ref.shape, F32)
    npair = nh // 2

    def attn_block(j, tie_carry):
        key = keys_ref[j]
        gt = key > thr
        eq = key == thr
        pref = _dot(ltri_ref[...], jnp.where(eq, 1.0, 0.0).astype(BF16))
        rank = tie_carry + pref
        adm = (j * kb + row_k) < limit
        sel = adm & (gt | (eq & (rank <= need)))
        bias = jnp.where(sel, 0.0, NEG_BIG)
        bias2 = jnp.concatenate([bias, bias], axis=-1)
        kvb = kv_ref[0, pl.ds(pl.multiple_of(j * kb, kb), kb), :]
        kvt = kvt_ref[0, j]
        hk = kb // 2
        hl = A_LORA // 2

        def logits(hp):
            qp = q_ref[0, 2 * hp:2 * hp + 2].reshape(2 * tq, KV_W)
            return jnp.concatenate([_dot_nt(kvb[0:hk], qp), _dot_nt(kvb[hk:kb], qp)], axis=0)

        st_next = logits(0)
        for hp in range(npair):
            st = st_next + bias2
            if hp + 1 < npair:
                st_next = logits(hp + 1)
            m_prev = m_ref[hp:hp + 1, :]
            m_new = jnp.maximum(m_prev, jnp.max(st, axis=0, keepdims=True))
            alpha = jnp.exp2(m_prev - m_new)
            p = jnp.exp2(st - m_new)
            l_ref[hp:hp + 1, :] = alpha * l_ref[hp:hp + 1, :] + jnp.sum(p, axis=0, keepdims=True)
            m_ref[hp:hp + 1, :] = m_new
            pb = p.astype(BF16)
            acc_ref[hp, 0:hl] = acc_ref[hp, 0:hl] * alpha + _dot(kvt[0:hl], pb)
            acc_ref[hp, hl:] = acc_ref[hp, hl:] * alpha + _dot(kvt[hl:], pb)
        return tie_carry + pref[kb - 1:kb, :]

    lax.fori_loop(0, nkb, attn_block, jnp.zeros((1, tq), F32))

    y = x_ref[0]
    for hp in range(npair):
        o_pair = (acc_ref[hp] / l_ref[hp:hp + 1, :]).T
        for e in range(2):
            y = y + _dot(o_pair[e * tq:(e + 1) * tq].astype(BF16), wov_ref[2 * hp + e])
    o_ref[0] = y


def _dsa_attn(x, q, qidx, widx, kidx, kv, kvt, w_ov, tq=128):
    b, t, d = x.shape
    kb = kvt.shape[-1]
    topk = min(TOPK_MAX, t // 4)
    tq = min(tq, t)
    assert t % kb == 0 and t % tq == 0 and tq % CHUNK == 0
    ltri = jnp.asarray(np.tril(np.ones((kb, kb), np.float32)), BF16)
    kern = functools.partial(_dsa_attn_kernel, tq=tq, kb=kb, topk=topk)
    per_batch = lambda shape: pl.BlockSpec(shape, lambda i, j: (i,) + (0,) * (len(shape) - 1),
                                           pipeline_mode=pl.Buffered(1))
    return pl.pallas_call(
        kern,
        grid=(b, t // tq),
        in_specs=[
            pl.BlockSpec((1, tq, d), lambda i, j: (i, j, 0)),
            pl.BlockSpec((1, A_HEADS, tq, KV_W), lambda i, j: (i, 0, j, 0)),
            pl.BlockSpec((1, tq, 256), lambda i, j: (i, j, 0)),
            pl.BlockSpec((1, 8, tq), lambda i, j: (i, 0, j)),
            per_batch((1, t, 256)),
            per_batch((1, t, KV_W)),
            per_batch((1, t // kb, A_LORA, kb)),
            _const_spec(w_ov.shape),
            _const_spec((kb, kb)),
        ],
        out_specs=pl.BlockSpec((1, tq, d), lambda i, j: (i, j, 0)),
        out_shape=jax.ShapeDtypeStruct((b, t, d), F32),
        scratch_shapes=[
            pltpu.VMEM((t // kb, kb, tq), I32),
            pltpu.VMEM((t // kb, kb, tq), I16),
            pltpu.VMEM((t // kb, kb, tq), I16),
            pltpu.VMEM((A_HEADS // 2, 2 * tq), F32),
            pltpu.VMEM((A_HEADS // 2, 2 * tq), F32),
            pltpu.VMEM((A_HEADS // 2, A_LORA, 2 * tq), F32),
        ],
        compiler_params=_cparams("parallel", "parallel"),
        name="dsa_attn",
    )(x, q, qidx, widx, kidx, kv, kvt, w_ov, ltri)


def _dsa_layer(x, positions, g_mix, w_in, norm_q, norm_kv, kidx_g, kidx_b, w_uq, w_uk, w_uv, w_qidx, w_o):
    scale = (A_NOPE + A_ROPE) ** -0.5 * math.log2(math.e)
    w_qlat, w_ov = _fold_weights(w_uq, w_uk, w_uv, w_o, scale)
    kv, kvt, kidx, qidx, widx, q = _dsa_proj(x, positions, g_mix, w_in, norm_q, norm_kv, kidx_g, kidx_b,
                                        w_uq, w_qidx, w_qlat, scale)
    return _dsa_attn(x, q, qidx, widx, kidx, kv, kvt, w_ov)


def _gdn_proj_kernel(x_ref, g_ref, wqkv_ref, wz_ref, wba_ref, conv_ref, alog_ref, dtb_ref, lblk_ref,
                     q_ref, k_ref, v_ref, z_ref, bg_ref, buf_ref, *, tm, cw):
    @pl.when(pl.program_id(1) == 0)
    def _():
        buf_ref[0:8, :] = jnp.zeros((8, buf_ref.shape[1]), F32)

    x = x_ref[0]
    h = _rms(x, g_ref[...]).astype(BF16)
    qk_w = B_QK_HEADS * B_HEAD
    ncol = wqkv_ref.shape[1]
    for c in range(ncol // cw):
        lo = c * cw
        cur = _dot(h, wqkv_ref[:, lo:lo + cw])
        buf_ref[8:8 + tm, lo:lo + cw] = cur
        y = cur * conv_ref[3:4, lo:lo + cw]
        for tap in range(B_CONV - 1):
            sh = B_CONV - 1 - tap
            y = y + buf_ref[8 - sh:8 - sh + tm, lo:lo + cw] * conv_ref[tap:tap + 1, lo:lo + cw]
        buf_ref[0:8, lo:lo + cw] = buf_ref[tm:tm + 8, lo:lo + cw]
        y = _silu(y)
        for s in range(cw // B_HEAD):
            col = lo + s * B_HEAD
            yh = y[:, s * B_HEAD:(s + 1) * B_HEAD]
            if col < 2 * qk_w:
                yh = yh * lax.rsqrt(jnp.sum(yh * yh, axis=-1, keepdims=True) + RMS_EPS)
                if col < qk_w:
                    q_ref[0, :, col:col + B_HEAD] = (yh * (B_HEAD ** -0.5)).astype(BF16)
                else:
                    k_ref[0, :, col - qk_w:col - qk_w + B_HEAD] = yh.astype(BF16)
            else:
                v_ref[0, :, col - 2 * qk_w:col - 2 * qk_w + B_HEAD] = yh.astype(BF16)
    z_ref[0] = _dot(h, wz_ref[...]).astype(BF16)
    ba = _dot(h, wba_ref[...])
    beta = jax.nn.sigmoid(ba)
    sp_in = ba + dtb_ref[...]
    softplus = jnp.maximum(sp_in, 0.0) + jnp.log1p(jnp.exp(-jnp.abs(sp_in)))
    g = -jnp.exp(alog_ref[...]) * softplus
    gcum = jnp.dot(lblk_ref[...], g, preferred_element_type=F32, precision=lax.Precision.HIGHEST)
    lane = lax.broadcasted_iota(I32, (tm, 128), 1)
    bg_ref[0] = jnp.where(lane < B_V_HEADS, beta, gcum)


def _gdn_proj(x, g_mix, w_in, conv_w, a_log, dt_bias, tm=256):
    b, t, d = x.shape
    tm = min(tm, t)
    qk_w = B_QK_HEADS * B_HEAD
    v_w = B_V_HEADS * B_HEAD
    cwid = 2 * qk_w + v_w
    wqkv = w_in[:, :cwid].astype(BF16)
    wz = w_in[:, cwid:cwid + v_w].astype(BF16)
    wba = jnp.concatenate([w_in[:, cwid + v_w:], jnp.zeros((d, 128 - 2 * B_V_HEADS), F32)], axis=-1).astype(BF16)
    pad = lambda v: jnp.concatenate([jnp.zeros((B_V_HEADS,), F32), v, jnp.zeros((128 - 2 * B_V_HEADS,), F32)]).reshape(1, 128)
    blk = np.arange(tm) // CHUNK
    lblk = jnp.asarray(((blk[:, None] == blk[None, :]) & (np.arange(tm)[:, None] >= np.arange(tm)[None, :])).astype(np.float32))
    kern = functools.partial(_gdn_proj_kernel, tm=tm, cw=512)
    tok = lambda w: pl.BlockSpec((1, tm, w), lambda i, j: (i, j, 0))
    return pl.pallas_call(
        kern,
        grid=(b, t // tm),
        in_specs=[
            tok(d), _const_spec((1, d)), _const_spec(wqkv.shape), _const_spec(wz.shape), _const_spec(wba.shape),
            _const_spec((B_CONV, cwid)), _const_spec((1, 128)), _const_spec((1, 128)), _const_spec((tm, tm)),
        ],
        out_specs=[tok(qk_w), tok(qk_w), tok(v_w), tok(v_w), tok(128)],
        out_shape=[
            jax.ShapeDtypeStruct((b, t, qk_w), BF16),
            jax.ShapeDtypeStruct((b, t, qk_w), BF16),
            jax.ShapeDtypeStruct((b, t, v_w), BF16),
            jax.ShapeDtypeStruct((b, t, v_w), BF16),
            jax.ShapeDtypeStruct((b, t, 128), F32),
        ],
        scratch_shapes=[pltpu.VMEM((tm + 8, cwid), F32)],
        compiler_params=_cparams("parallel", "arbitrary"),
        name="gdn_proj",
    )(x, g_mix.reshape(1, d), wqkv, wz, wba, conv_w, pad(a_log), pad(dt_bias), lblk)


def _gdn_prep_kernel(q_ref, k_ref, v_ref, bg_ref, u_ref, w_ref, qg_ref, kd_ref, aqk_ref, eg_ref, *, tp):
    hv = pl.program_id(2)
    c = CHUNK
    ri = lax.broadcasted_iota(I32, (c, c), 0)
    ci = lax.broadcasted_iota(I32, (c, c), 1)
    eye = ri == ci
    lane = lax.broadcasted_iota(I32, (c, 128), 1)
    rowi = lax.broadcasted_iota(I32, (c, 1), 0)
    sls = [slice(n * c, (n + 1) * c) for n in range(tp // c)]
    bcols, gcols, decays, invs, xps = [], [], [], [], []
    for n, sl in enumerate(sls):
        bg = bg_ref[0, sl, :]
        bcol = jnp.sum(jnp.where(lane == hv, bg, 0.0), axis=-1, keepdims=True)
        gcol = jnp.sum(jnp.where(lane == hv + B_V_HEADS, bg, 0.0), axis=-1, keepdims=True)
        grow = jnp.sum(jnp.where(eye, gcol, 0.0), axis=0, keepdims=True)
        glast = jnp.sum(jnp.where(rowi == c - 1, gcol, 0.0), axis=0, keepdims=True)
        kb_ = k_ref[0, sl, :]
        qb_ = q_ref[0, sl, :]
        decay = jnp.exp(jnp.where(ri >= ci, gcol - grow, -jnp.inf))
        x_ = jnp.where(ri > ci, -(bcol * _dot_nt(kb_, kb_) * decay), 0.0)
        qg_ref[0, sl, :] = (qb_.astype(F32) * jnp.exp(gcol)).astype(BF16)
        kd_ref[0, sl, :] = (kb_.astype(F32) * jnp.exp(glast - gcol)).astype(BF16)
        aqk = jnp.where(ri >= ci, _dot_nt(qb_, kb_) * decay, 0.0)
        aqk_ref[0, sl, :] = jnp.concatenate([aqk, jnp.zeros_like(aqk)], axis=-1).astype(BF16)
        eg_ref[0, 0, n:n + 1, :] = jnp.broadcast_to(jnp.exp(glast), (1, 128))
        bcols.append(bcol)
        gcols.append(gcol)
        xps.append(x_)
        invs.append(jnp.where(eye, 1.0, 0.0) + x_)
    for _ in range(5):
        xbs = [xp.astype(BF16) for xp in xps]
        xps = [_dot(xb, xb) for xb in xbs]
        invs = [inv + _dot(inv.astype(BF16), xp.astype(BF16)) for inv, xp in zip(invs, xps)]
    for sl, inv, bcol, gcol in zip(sls, invs, bcols, gcols):
        rhs = jnp.concatenate([v_ref[0, sl, :].astype(F32) * bcol,
                               k_ref[0, sl, :].astype(F32) * (bcol * jnp.exp(gcol))], axis=-1)
        uw = _dot(inv.astype(BF16), rhs.astype(BF16))
        u_ref[0, sl, :] = uw[:, :B_HEAD]
        w_ref[0, sl, :] = uw[:, B_HEAD:].astype(BF16)


def _gdn_prep(q, k, v, bg, tp=1024):
    b, t, _ = q.shape
    tp = min(tp, t)
    nv = B_V_HEADS
    rep = B_V_HEADS // B_QK_HEADS
    v_w = nv * B_HEAD
    hspec = lambda: pl.BlockSpec((1, tp, B_HEAD), lambda i, j, h: (i, j, h))
    qkspec = lambda: pl.BlockSpec((1, tp, B_HEAD), lambda i, j, h: (i, j, h // rep))
    return pl.pallas_call(
        functools.partial(_gdn_prep_kernel, tp=tp),
        grid=(b, t // tp, nv),
        in_specs=[qkspec(), qkspec(), hspec(), pl.BlockSpec((1, tp, 128), lambda i, j, h: (i, j, 0))],
        out_specs=[hspec(), hspec(), hspec(), hspec(), hspec(),
                   pl.BlockSpec((1, 1, tp // CHUNK, 128), lambda i, j, h: (i, h, j, 0))],
        out_shape=[
            jax.ShapeDtypeStruct((b, t, v_w), F32),
            jax.ShapeDtypeStruct((b, t, v_w), BF16),
            jax.ShapeDtypeStruct((b, t, v_w), BF16),
            jax.ShapeDtypeStruct((b, t, v_w), BF16),
            jax.ShapeDtypeStruct((b, t, v_w), BF16),
            jax.ShapeDtypeStruct((b, nv, t // CHUNK, 128), F32),
        ],
        compiler_params=_cparams("parallel", "parallel", "parallel"),
        name="gdn_prep",
    )(q, k, v, bg)


def _gdn_scan_kernel(u_ref, w_ref, qg_ref, kd_ref, aqk_ref, eg_ref, z_ref, no_ref, o_ref, state_ref, *, tt, hb):
    @pl.when(pl.program_id(2) == 0)
    def _():
        state_ref[...] = jnp.zeros(state_ref.shape, F32)

    c = CHUNK

    def chunk_body(n, carry):
        r0 = pl.multiple_of(n * c, c)
        rs = pl.ds(r0, c)
        lss = [slice(hh * B_HEAD, (hh + 1) * B_HEAD) for hh in range(hb)]
        sts = [state_ref[hh] for hh in range(hb)]
        stbs = [st.astype(BF16) for st in sts]
        vbs = [(u_ref[0, rs, ls] - _dot(w_ref[0, rs, ls], stb)).astype(BF16) for ls, stb in zip(lss, stbs)]
        os_ = [_dot(qg_ref[0, rs, ls], stb) + _dot(aqk_ref[0, rs, ls][:, 0:c], vb)
               for ls, stb, vb in zip(lss, stbs, vbs)]
        for hh, (ls, st, vb) in enumerate(zip(lss, sts, vbs)):
            eg = eg_ref[0, hh, pl.ds(n, 1), :]
            upd = lax.dot_general(kd_ref[0, rs, ls], vb, (((0,), (0,)), ((), ())), preferred_element_type=F32)
            state_ref[hh] = st * eg + upd
        for ls, o in zip(lss, os_):
            on = o * lax.rsqrt(jnp.mean(o * o, axis=-1, keepdims=True) + RMS_EPS) * no_ref[...]
            o_ref[0, rs, ls] = (on * _silu(z_ref[0, rs, ls].astype(F32))).astype(BF16)
        return carry

    lax.fori_loop(0, tt // c, chunk_body, 0)


def _gdn_scan(u, w, qg, kd, aqk, eg, z, norm_o, tt=1024, hb=8):
    b, t, v_w = u.shape
    tt = min(tt, t)
    nv = B_V_HEADS
    wid = hb * B_HEAD
    spec = lambda: pl.BlockSpec((1, tt, wid), lambda i, g, j: (i, j, g))
    return pl.pallas_call(
        functools.partial(_gdn_scan_kernel, tt=tt, hb=hb),
        grid=(b, nv // hb, t // tt),
        in_specs=[spec(), spec(), spec(), spec(), spec(),
                  pl.BlockSpec((1, hb, tt // CHUNK, 128), lambda i, g, j: (i, g, j, 0)),
                  spec(), _const_spec((1, B_HEAD))],
        out_specs=spec(),
        out_shape=jax.ShapeDtypeStruct((b, t, v_w), BF16),
        scratch_shapes=[pltpu.VMEM((hb, B_HEAD, B_HEAD), F32)],
        compiler_params=_cparams("parallel", "parallel", "arbitrary"),
        name="gdn_scan",
    )(u, w, qg, kd, aqk, eg, z, norm_o.reshape(1, B_HEAD))


def _proj_res_kernel(x_ref, a_ref, w_ref, o_ref):
    o_ref[...] = x_ref[...] + _dot(a_ref[...], w_ref[...])


def _proj_res(x2, a2, w, tm=512):
    n, d = x2.shape
    kdim = a2.shape[1]
    tm = min(tm, n)
    return pl.pallas_call(
        _proj_res_kernel,
        grid=(n // tm,),
        in_specs=[pl.BlockSpec((tm, d), lambda i: (i, 0)), pl.BlockSpec((tm, kdim), lambda i: (i, 0)),
                  _const_spec((kdim, d))],
        out_specs=pl.BlockSpec((tm, d), lambda i: (i, 0)),
        out_shape=jax.ShapeDtypeStruct((n, d), F32),
        compiler_params=_cparams("parallel"),
        name="proj_res",
    )(x2, a2, w.astype(BF16))


def _gdn_layer(x, g_mix, w_in, conv_w, a_log, dt_bias, norm_o, w_o):
    b, t, d = x.shape
    q, k, v, z, bg = _gdn_proj(x, g_mix, w_in, conv_w, a_log, dt_bias)
    u, w, qg, kd, aqk, eg = _gdn_prep(q, k, v, bg)
    o = _gdn_scan(u, w, qg, kd, aqk, eg, z, norm_o)
    return _proj_res(x.reshape(b * t, d), o.reshape(b * t, -1), w_o).reshape(b, t, d)


def kernel(x, mem, positions, ffn1_norm, ffn1_w_in, ffn1_w_out, mix_norm, xattn_norm, mem_norm, xattn_w_q, xattn_w_kv, xattn_w_o, ffn2_norm, ffn2_w_in, ffn2_w_out, a_w_in, a_norm_q, a_norm_kv, a_kidx_g, a_kidx_b, a_w_uq, a_w_uk, a_w_uv, a_w_qidx, a_w_o, b_w_in, b_conv, b_a_log, b_dt_bias, b_norm_o, b_w_o, final_norm):
    b, t, d = x.shape
    depth = ffn1_norm.shape[0]
    n_mixers = 2
    for i in range(depth):
        x = _ffn(x.reshape(b * t, d), ffn1_norm[i], ffn1_w_in[i], ffn1_w_out[i]).reshape(b, t, d)
        j = i // n_mixers
        if i % n_mixers == 0:
            x = _dsa_layer(x, positions, mix_norm[i], a_w_in[j], a_norm_q[j], a_norm_kv[j], a_kidx_g[j],
                           a_kidx_b[j], a_w_uq[j], a_w_uk[j], a_w_uv[j], a_w_qidx[j], a_w_o[j])
        else:
            x = _gdn_layer(x, mix_norm[i], b_w_in[j], b_conv[j], b_a_log[j], b_dt_bias[j], b_norm_o[j], b_w_o[j])
        x = _xattn(x, mem, xattn_norm[i], mem_norm[i], xattn_w_q[i], xattn_w_kv[i], xattn_w_o[i])
        last = i == depth - 1
        x = _ffn(x.reshape(b * t, d), ffn2_norm[i], ffn2_w_in[i], ffn2_w_out[i],
                 final_g=final_norm if last else None).reshape(b, t, d)
    return x
```

```python
import functools
import math

import jax
import jax.numpy as jnp
import numpy as np
from jax import lax
from jax.experimental import pallas as pl
from jax.experimental.pallas import tpu as pltpu

F32 = jnp.float32
BF16 = jnp.bfloat16
I32 = jnp.int32

RMS_EPS = 1e-6
ROPE_THETA = 10000.0
CHUNK = 64
TOPK_MAX = 256
A_HEADS = 16
A_LORA = 256
A_NOPE = 64
A_ROPE = 32
A_V = 64
IDX_HEADS = 4
IDX_DIM = 64
KV_W = 384
B_QK_HEADS = 8
B_V_HEADS = 16
B_HEAD = 128
B_CONV = 4
X_HEADS = 4
NEG_BIG = -1e30
INT_MIN = -2147483648
KEY_NEG_INF = -2139095041

VMEM_LIMIT_BYTES = 58 * 1024 * 1024


def _cparams(*sem):
    return pltpu.CompilerParams(dimension_semantics=sem, vmem_limit_bytes=VMEM_LIMIT_BYTES)


def _const_spec(shape):
    nd = len(shape)
    return pl.BlockSpec(shape, lambda *_: (0,) * nd, pipeline_mode=pl.Buffered(1))


def _rms(x, g):
    return x * lax.rsqrt(jnp.mean(x * x, axis=-1, keepdims=True) + RMS_EPS) * g


def _dot(a, b):
    return jnp.dot(a, b, preferred_element_type=F32)


def _dot_nt(a, b):
    return lax.dot_general(a, b, (((1,), (1,)), ((), ())), preferred_element_type=F32)


def _silu(x):
    return x * jax.nn.sigmoid(x)


def _ffn_kernel(x_ref, g_ref, win_ref, wout_ref, fg_ref, o_ref, acc_ref, *, d_ff, f_chunk, final_norm):
    x = x_ref[...]
    h = _rms(x, g_ref[...]).astype(BF16)
    for c in range(d_ff // f_chunk):
        lo = c * f_chunk
        gate = _dot(h, win_ref[:, lo:lo + f_chunk])
        up = _dot(h, win_ref[:, d_ff + lo:d_ff + lo + f_chunk])
        a = (_silu(gate) * up).astype(BF16)
        y = _dot(a, wout_ref[lo:lo + f_chunk, :])
        if c == 0:
            acc_ref[...] = y
        else:
            acc_ref[...] += y
    out = x + 0.5 * acc_ref[...]
    if final_norm:
        out = _rms(out, fg_ref[...])
    o_ref[...] = out


def _ffn(x2, g, w_in, w_out, final_g=None, tm=512):
    n, d = x2.shape
    d_ff = w_out.shape[0]
    f_chunk = 256
    assert n % tm == 0 and d_ff % f_chunk == 0
    fg = final_g if final_g is not None else g
    kern = functools.partial(_ffn_kernel, d_ff=d_ff, f_chunk=f_chunk, final_norm=final_g is not None)
    return pl.pallas_call(
        kern,
        grid=(n // tm,),
        in_specs=[
            pl.BlockSpec((tm, d), lambda i: (i, 0)),
            _const_spec((1, d)),
            _const_spec((d, 2 * d_ff)),
            _const_spec((d_ff, d)),
            _const_spec((1, d)),
        ],
        out_specs=pl.BlockSpec((tm, d), lambda i: (i, 0)),
        out_shape=jax.ShapeDtypeStruct((n, d), F32),
        scratch_shapes=[pltpu.VMEM((tm, d), F32)],
        compiler_params=_cparams("parallel"),
        name="ffn",
    )(x2, g.reshape(1, d), w_in.astype(BF16), w_out.astype(BF16), fg.reshape(1, d))


def _mem_kv_kernel(mem_ref, g_ref, wkv_ref, k_ref, v_ref, *, d):
    m = _rms(mem_ref[0], g_ref[...]).astype(BF16)
    kv = _dot(m, wkv_ref[...])
    k_ref[0] = kv[:, :d].astype(BF16)
    v_ref[0] = kv[:, d:].astype(BF16)


def _xattn_kernel(x_ref, g_ref, wq_ref, k_ref, v_ref, wo_ref, o_ref, *, heads):
    x = x_ref[0]
    d = x.shape[-1]
    hd = d // heads
    h = _rms(x, g_ref[...]).astype(BF16)
    q = (_dot(h, wq_ref[...]) * (hd ** -0.5)).astype(BF16)
    outs = []
    for i in range(heads):
        s = _dot_nt(q[:, i * hd:(i + 1) * hd], k_ref[0, :, i * hd:(i + 1) * hd])
        m = jnp.max(s, axis=-1, keepdims=True)
        p = jnp.exp(s - m)
        l = jnp.sum(p, axis=-1, keepdims=True)
        o = _dot(p.astype(BF16), v_ref[0, :, i * hd:(i + 1) * hd])
        outs.append((o / l).astype(BF16))
    o = jnp.concatenate(outs, axis=-1)
    o_ref[0] = x + _dot(o, wo_ref[...])


def _xattn(x, mem, g_x, g_mem, w_q, w_kv, w_o, tm=512):
    b, t, d = x.shape
    ml = mem.shape[1]
    k, v = pl.pallas_call(
        functools.partial(_mem_kv_kernel, d=d),
        grid=(b,),
        in_specs=[
            pl.BlockSpec((1, ml, d), lambda i: (i, 0, 0)),
            _const_spec((1, d)),
            _const_spec((d, 2 * d)),
        ],
        out_specs=[pl.BlockSpec((1, ml, d), lambda i: (i, 0, 0))] * 2,
        out_shape=[jax.ShapeDtypeStruct((b, ml, d), BF16)] * 2,
        compiler_params=_cparams("parallel"),
        name="mem_kv",
    )(mem, g_mem.reshape(1, d), w_kv.astype(BF16))
    tm = min(tm, t)
    return pl.pallas_call(
        functools.partial(_xattn_kernel, heads=X_HEADS),
        grid=(b, t // tm),
        in_specs=[
            pl.BlockSpec((1, tm, d), lambda i, j: (i, j, 0)),
            _const_spec((1, d)),
            _const_spec((d, d)),
            pl.BlockSpec((1, ml, d), lambda i, j: (i, 0, 0)),
            pl.BlockSpec((1, ml, d), lambda i, j: (i, 0, 0)),
            _const_spec((d, d)),
        ],
        out_specs=pl.BlockSpec((1, tm, d), lambda i, j: (i, j, 0)),
        out_shape=jax.ShapeDtypeStruct((b, t, d), F32),
        compiler_params=_cparams("parallel", "parallel"),
        name="xattn",
    )(x, g_x.reshape(1, d), w_q.astype(BF16), k, v, w_o.astype(BF16))


def _fold_nt_kernel(a_ref, b_ref, o_ref, *, scale):
    o = lax.dot_general(a_ref[0], b_ref[0], (((1,), (1,)), ((), ())),
                        preferred_element_type=F32, precision=lax.Precision.HIGHEST)
    o_ref[...] = (o * scale).astype(o_ref.dtype)


def _fold_nn_kernel(a_ref, b_ref, o_ref):
    o = jnp.dot(a_ref[0], b_ref[0], preferred_element_type=F32, precision=lax.Precision.HIGHEST)
    o_ref[0] = o.astype(o_ref.dtype)


def _fold_weights(w_uq, w_uk, w_uv, w_o, scale):
    nh = A_HEADS
    uq = w_uq.reshape(A_LORA, nh, A_NOPE + A_ROPE)
    uq_nope = jnp.transpose(uq[:, :, :A_NOPE], (1, 0, 2))
    w_qlat = pl.pallas_call(
        functools.partial(_fold_nt_kernel, scale=scale),
        grid=(nh,),
        in_specs=[pl.BlockSpec((1, A_LORA, A_NOPE), lambda h: (h, 0, 0)),
                  pl.BlockSpec((1, A_LORA, A_NOPE), lambda h: (h, 0, 0))],
        out_specs=pl.BlockSpec((A_LORA, A_LORA), lambda h: (0, h)),
        out_shape=jax.ShapeDtypeStruct((A_LORA, nh * A_LORA), BF16),
        compiler_params=_cparams("parallel"),
        name="fold_qlat",
    )(uq_nope, w_uk)
    d = w_o.shape[1]
    w_ov = pl.pallas_call(
        _fold_nn_kernel,
        grid=(nh,),
        in_specs=[pl.BlockSpec((1, A_LORA, A_V), lambda h: (h, 0, 0)),
                  pl.BlockSpec((1, A_V, d), lambda h: (h, 0, 0))],
        out_specs=pl.BlockSpec((1, A_LORA, d), lambda h: (h, 0, 0)),
        out_shape=jax.ShapeDtypeStruct((nh, A_LORA, d), BF16),
        compiler_params=_cparams("parallel"),
        name="fold_ov",
    )(w_uv, w_o.reshape(nh, A_V, d))
    return w_qlat, w_ov


def _dsa_proj_kernel(x_ref, pos_ref, g_ref, wa_ref, nq_ref, nkv_ref, lng_ref, lnb_ref, inv64_ref, inv32_ref,
                     wqi_ref, wpe_ref, wql_ref,
                     kv_ref, kvt_ref, kidx_ref, qidx_ref, widx_ref, q_ref):
    x = x_ref[0]
    tm = x.shape[0]
    h = _rms(x, g_ref[...]).astype(BF16)
    p = _dot(h, wa_ref[...])
    cq = _rms(p[:, 0:256], nq_ref[...])
    ckv = _rms(p[:, 256:512], nkv_ref[...])
    posf = pos_ref[0].astype(F32)
    a64 = posf * inv64_ref[...]
    c64, s64 = jnp.cos(a64), jnp.sin(a64)
    a32 = posf * inv32_ref[...]
    c32, s32 = jnp.cos(a32), jnp.sin(a32)
    lane = lax.broadcasted_iota(I32, (tm, 128), 1)

    g2 = p[:, 512:640]
    kr = g2 * c32 + pltpu.roll(g2, 96, 1) * s32
    kr = jnp.where(lane < A_ROPE, kr, 0.0)
    kv_ref[0, :, 0:256] = ckv.astype(BF16)
    kvt_ref[0, 0] = ckv.T.astype(BF16)
    kv_ref[0, :, 256:384] = kr.astype(BF16)

    g3 = p[:, 640:768]
    valid = lane < IDX_DIM
    mu = jnp.sum(jnp.where(valid, g3, 0.0), axis=-1, keepdims=True) * (1.0 / IDX_DIM)
    dlt = g3 - mu
    var = jnp.sum(jnp.where(valid, dlt * dlt, 0.0), axis=-1, keepdims=True) * (1.0 / IDX_DIM)
    y = dlt * lax.rsqrt(var + RMS_EPS) * lng_ref[...] + lnb_ref[...]
    ki = y * c64 + pltpu.roll(y, 64, 1) * s64
    ki = jnp.where(valid, ki, pltpu.roll(ki, 64, 1))
    ki = ki.astype(BF16)
    kidx_ref[0] = jnp.concatenate([ki, ki], axis=-1)

    g4 = p[:, 768:896] * (IDX_HEADS ** -0.5)
    widx_ref[0] = g4.T[0:8, :]

    cqb = cq.astype(BF16)
    c64x2 = jnp.concatenate([c64, c64], axis=-1)
    s64x2 = jnp.concatenate([s64, s64], axis=-1)
    qi = _dot(cqb, wqi_ref[:, 0:256]) * c64x2 + _dot(cqb, wqi_ref[:, 256:512]) * s64x2
    qidx_ref[0] = qi.astype(BF16)

    c32x4 = jnp.concatenate([c32] * 4, axis=-1)
    s32x4 = jnp.concatenate([s32] * 4, axis=-1)
    qpe = _dot(cqb, wpe_ref[:, 0:512]) * c32x4 + _dot(cqb, wpe_ref[:, 512:1024]) * s32x4
    for hh in range(A_HEADS):
        ql = _dot(cqb, wql_ref[:, hh * 256:(hh + 1) * 256])
        q_ref[0, hh, :, 0:256] = ql.astype(BF16)
        tile = qpe[:, (hh // 4) * 128:(hh // 4 + 1) * 128]
        off = (hh % 4) * A_ROPE
        if off:
            tile = pltpu.roll(tile, 128 - off, 1)
        q_ref[0, hh, :, 256:384] = jnp.where(lane < A_ROPE, tile, 0.0).astype(BF16)


def _rot_half_cols(w, width):
    k = w.shape[0]
    wg = w.reshape(k, -1, width)
    half = A_ROPE // 2
    sw = jnp.concatenate([-wg[:, :, half:A_ROPE], wg[:, :, :half],
                          jnp.zeros((k, wg.shape[1], width - A_ROPE), w.dtype)], axis=-1)
    return sw.reshape(k, -1)


def _dsa_proj(x, positions, g_mix, w_in, norm_q, norm_kv, kidx_g, kidx_b, w_uq, w_qidx, w_qlat, scale, tm=512):
    b, t, d = x.shape
    tm = min(tm, t)
    half = A_ROPE // 2
    o_kr = 2 * A_LORA
    o_ki = o_kr + A_ROPE
    o_w = o_ki + IDX_DIM
    zeros = lambda n: jnp.zeros((d, n), F32)
    w_kr = w_in[:, o_kr:o_ki]
    w_ki = w_in[:, o_ki:o_w]
    w_ki_perm = jnp.concatenate([w_ki[:, half:A_ROPE], w_ki[:, :half]], axis=-1)
    wa = jnp.concatenate([
        w_in[:, :o_kr],
        w_kr, _rot_half_cols(w_kr, A_ROPE), zeros(64),
        w_ki, w_ki_perm, zeros(32),
        w_in[:, o_w:o_w + IDX_HEADS], zeros(128 - IDX_HEADS)], axis=-1).astype(BF16)
    sgn = jnp.concatenate([-jnp.ones((half,), F32), jnp.ones((half,), F32)])
    perm = lambda v: jnp.concatenate([v[half:A_ROPE], v[:half]])
    lng = jnp.concatenate([kidx_g, sgn * perm(kidx_g), jnp.zeros((32,), F32)]).reshape(1, 128)
    lnb = jnp.concatenate([kidx_b, sgn * perm(kidx_b), jnp.zeros((32,), F32)]).reshape(1, 128)
    inv = ROPE_THETA ** (-jnp.arange(0, A_ROPE, 2, dtype=F32) / A_ROPE)
    inv64 = jnp.tile(jnp.concatenate([inv, inv, jnp.zeros((32,), F32)]), 2).reshape(1, 128)
    inv32 = jnp.tile(inv, 8).reshape(1, 128)
    wqi = jnp.concatenate([w_qidx, _rot_half_cols(w_qidx, IDX_DIM)], axis=-1).astype(BF16)
    uq = w_uq.reshape(A_LORA, A_HEADS, A_NOPE + A_ROPE)
    w_pe = (uq[:, :, A_NOPE:] * scale).reshape(A_LORA, A_HEADS * A_ROPE)
    wpe = jnp.concatenate([w_pe, _rot_half_cols(w_pe, A_ROPE)], axis=-1).astype(BF16)
    row = lambda a: a.reshape(1, -1)
    outs = pl.pallas_call(
        _dsa_proj_kernel,
        grid=(b, t // tm),
        in_specs=[
            pl.BlockSpec((1, tm, d), lambda i, j: (i, j, 0)),
            pl.BlockSpec((1, tm, 1), lambda i, j: (i, j, 0)),
            _const_spec((1, d)),
            _const_spec(wa.shape),
            _const_spec((1, A_LORA)), _const_spec((1, A_LORA)),
            _const_spec((1, 128)), _const_spec((1, 128)), _const_spec((1, 128)), _const_spec((1, 128)),
            _const_spec(wqi.shape), _const_spec(wpe.shape), _const_spec(w_qlat.shape),
        ],
        out_specs=[
            pl.BlockSpec((1, tm, KV_W), lambda i, j: (i, j, 0)),
            pl.BlockSpec((1, 1, A_LORA, tm), lambda i, j: (i, j, 0, 0)),
            pl.BlockSpec((1, tm, 256), lambda i, j: (i, j, 0)),
            pl.BlockSpec((1, tm, 256), lambda i, j: (i, j, 0)),
            pl.BlockSpec((1, 8, tm), lambda i, j: (i, 0, j)),
            pl.BlockSpec((1, A_HEADS, tm, KV_W), lambda i, j: (i, 0, j, 0)),
        ],
        out_shape=[
            jax.ShapeDtypeStruct((b, t, KV_W), BF16),
            jax.ShapeDtypeStruct((b, t // tm, A_LORA, tm), BF16),
            jax.ShapeDtypeStruct((b, t, 256), BF16),
            jax.ShapeDtypeStruct((b, t, 256), BF16),
            jax.ShapeDtypeStruct((b, 8, t), F32),
            jax.ShapeDtypeStruct((b, A_HEADS, t, KV_W), BF16),
        ],
        compiler_params=_cparams("parallel", "parallel"),
        name="dsa_proj",
    )(x, positions.reshape(b, t, 1), row(g_mix), wa, row(norm_q), row(norm_kv), lng, lnb, inv64, inv32,
      wqi, wpe, w_qlat)
    return outs


def _dsa_attn_kernel(x_ref, q_ref, qidx_ref, widx_ref, kidx_ref, kv_ref, kvt_ref, wov_ref, ltri_ref,
                     o_ref, keys_ref, m_ref, l_ref, acc_ref, *, tq, kb, topk):
    i = pl.program_id(1)
    t0 = i * tq
    nkb = (t0 + tq + kb - 1) // kb
    nh = A_HEADS
    rows = nh * tq

    lane_q = lax.broadcasted_iota(I32, (1, tq), 1)
    limit = ((t0 + lane_q) // CHUNK + 1) * CHUNK
    row_k = lax.broadcasted_iota(I32, (kb, tq), 0)

    qi = qidx_ref[0]
    lane256 = lax.broadcasted_iota(I32, (tq, 256), 1)
    qcat = jnp.concatenate([jnp.where((lane256 // IDX_DIM) == hh, qi, jnp.zeros_like(qi))
                            for hh in range(IDX_HEADS)], axis=0)
    wrows = [widx_ref[0, hh:hh + 1, :] for hh in range(IDX_HEADS)]

    def score_block(j, carry):
        kblk = kidx_ref[0, pl.ds(pl.multiple_of(j * kb, kb), kb), :]
        logit = _dot_nt(kblk, qcat)
        sc = jnp.zeros((kb, tq), F32)
        for hh in range(IDX_HEADS):
            sc = sc + jnp.maximum(logit[:, hh * tq:(hh + 1) * tq], 0.0) * wrows[hh]
        sc = sc * (IDX_DIM ** -0.5)
        bits = lax.bitcast_convert_type(sc, I32)
        bits = jnp.where(bits == INT_MIN, 0, bits)
        key = bits ^ ((bits >> 31) & 0x7FFFFFFF)
        adm = (j * kb + row_k) < limit
        keys_ref[j] = jnp.where(adm, key, KEY_NEG_INF)
        return carry

    lax.fori_loop(0, nkb, score_block, 0)

    def count(pred_fn):
        def body(j, c):
            hit = jnp.where(pred_fn(keys_ref[j]), 1, 0).astype(I32)
            return c + jnp.sum(hit.reshape(kb // 8, 8, tq), axis=0)
        c = lax.fori_loop(0, nkb, body, jnp.zeros((8, tq), I32))
        return jnp.sum(c, axis=0, keepdims=True)

    def bit_step(bi, thr):
        cand = thr + jnp.left_shift(jnp.int32(1), 31 - bi)
        cnt = count(lambda k: k >= cand)
        return jnp.where(cnt >= topk, cand, thr)

    thr = lax.fori_loop(0, 32, bit_step, jnp.full((1, tq), INT_MIN, I32))
    need = (topk - count(lambda k: k > thr)).astype(F32)

    m_ref[...] = jnp.full(m_ref.shape, -jnp.inf, F32)
    l_ref[...] = jnp.zeros(l_ref.shape, F32)
    acc_ref[...] = jnp.zeros(acc_ref.shape, F32)
    npair = nh // 2

    def attn_block(j, tie_carry):
        key = keys_ref[j]
        gt = key > thr
        eq = key == thr
        pref = _dot(ltri_ref[...], jnp.where(eq, 1.0, 0.0).astype(BF16))
        rank = tie_carry + pref
        adm = (j * kb + row_k) < limit
        sel = adm & (gt | (eq & (rank <= need)))
        bias = jnp.where(sel, 0.0, NEG_BIG)
        bias2 = jnp.concatenate([bias, bias], axis=-1)
        kvb = kv_ref[0, pl.ds(pl.multiple_of(j * kb, kb), kb), :]
        kvt = kvt_ref[0, j]
        hk = kb // 2
        hl = A_LORA // 2

        def logits(hp):
            qp = q_ref[0, 2 * hp:2 * hp + 2].reshape(2 * tq, KV_W)
            return jnp.concatenate([_dot_nt(kvb[0:hk], qp), _dot_nt(kvb[hk:kb], qp)], axis=0)

        st_next = logits(0)
        for hp in range(npair):
            st = st_next + bias2
            if hp + 1 < npair:
                st_next = logits(hp + 1)
            m_prev = m_ref[hp:hp + 1, :]
            m_new = jnp.maximum(m_prev, jnp.max(st, axis=0, keepdims=True))
            alpha = jnp.exp2(m_prev - m_new)
            p = jnp.exp2(st - m_new)
            l_ref[hp:hp + 1, :] = alpha * l_ref[hp:hp + 1, :] + jnp.sum(p, axis=0, keepdims=True)
            m_ref[hp:hp + 1, :] = m_new
            pb = p.astype(BF16)
            acc_ref[hp, 0:hl] = acc_ref[hp, 0:hl] * alpha + _dot(kvt[0:hl], pb)
            acc_ref[hp, hl:] = acc_ref[hp, hl:] * alpha + _dot(kvt[hl:], pb)
        return tie_carry + pref[kb - 1:kb, :]

    lax.fori_loop(0, nkb, attn_block, jnp.zeros((1, tq), F32))

    y = x_ref[0]
    for hp in range(npair):
        o_pair = (acc_ref[hp] / l_ref[hp:hp + 1, :]).T
        for e in range(2):
            y = y + _dot(o_pair[e * tq:(e + 1) * tq].astype(BF16), wov_ref[2 * hp + e])
    o_ref[0] = y


def _dsa_attn(x, q, qidx, widx, kidx, kv, kvt, w_ov, tq=128):
    b, t, d = x.shape
    kb = kvt.shape[-1]
    topk = min(TOPK_MAX, t // 4)
    tq = min(tq, t)
    assert t % kb == 0 and t % tq == 0 and tq % CHUNK == 0
    ltri = jnp.asarray(np.tril(np.ones((kb, kb), np.float32)), BF16)
    kern = functools.partial(_dsa_attn_kernel, tq=tq, kb=kb, topk=topk)
    per_batch = lambda shape: pl.BlockSpec(shape, lambda i, j: (i,) + (0,) * (len(shape) - 1),
                                           pipeline_mode=pl.Buffered(1))
    return pl.pallas_call(
        kern,
        grid=(b, t // tq),
        in_specs=[
            pl.BlockSpec((1, tq, d), lambda i, j: (i, j, 0)),
            pl.BlockSpec((1, A_HEADS, tq, KV_W), lambda i, j: (i, 0, j, 0)),
            pl.BlockSpec((1, tq, 256), lambda i, j: (i, j, 0)),
            pl.BlockSpec((1, 8, tq), lambda i, j: (i, 0, j)),
            per_batch((1, t, 256)),
            per_batch((1, t, KV_W)),
            per_batch((1, t // kb, A_LORA, kb)),
            _const_spec(w_ov.shape),
            _const_spec((kb, kb)),
        ],
        out_specs=pl.BlockSpec((1, tq, d), lambda i, j: (i, j, 0)),
        out_shape=jax.ShapeDtypeStruct((b, t, d), F32),
        scratch_shapes=[
            pltpu.VMEM((t // kb, kb, tq), I32),
            pltpu.VMEM((A_HEADS // 2, 2 * tq), F32),
            pltpu.VMEM((A_HEADS // 2, 2 * tq), F32),
            pltpu.VMEM((A_HEADS // 2, A_LORA, 2 * tq), F32),
        ],
        compiler_params=_cparams("parallel", "parallel"),
        name="dsa_attn",
    )(x, q, qidx, widx, kidx, kv, kvt, w_ov, ltri)


def _dsa_layer(x, positions, g_mix, w_in, norm_q, norm_kv, kidx_g, kidx_b, w_uq, w_uk, w_uv, w_qidx, w_o):
    scale = (A_NOPE + A_ROPE) ** -0.5 * math.log2(math.e)
    w_qlat, w_ov = _fold_weights(w_uq, w_uk, w_uv, w_o, scale)
    kv, kvt, kidx, qidx, widx, q = _dsa_proj(x, positions, g_mix, w_in, norm_q, norm_kv, kidx_g, kidx_b,
                                        w_uq, w_qidx, w_qlat, scale)
    return _dsa_attn(x, q, qidx, widx, kidx, kv, kvt, w_ov)


def _gdn_proj_kernel(x_ref, g_ref, wqkv_ref, wz_ref, wba_ref, conv_ref, alog_ref, dtb_ref, lblk_ref,
                     q_ref, k_ref, v_ref, z_ref, bg_ref, buf_ref, *, tm, cw):
    @pl.when(pl.program_id(1) == 0)
    def _():
        buf_ref[0:8, :] = jnp.zeros((8, buf_ref.shape[1]), F32)

    x = x_ref[0]
    h = _rms(x, g_ref[...]).astype(BF16)
    qk_w = B_QK_HEADS * B_HEAD
    ncol = wqkv_ref.shape[1]
    for c in range(ncol // cw):
        lo = c * cw
        cur = _dot(h, wqkv_ref[:, lo:lo + cw])
        buf_ref[8:8 + tm, lo:lo + cw] = cur
        y = cur * conv_ref[3:4, lo:lo + cw]
        for tap in range(B_CONV - 1):
            sh = B_CONV - 1 - tap
            y = y + buf_ref[8 - sh:8 - sh + tm, lo:lo + cw] * conv_ref[tap:tap + 1, lo:lo + cw]
        buf_ref[0:8, lo:lo + cw] = buf_ref[tm:tm + 8, lo:lo + cw]
        y = _silu(y)
        for s in range(cw // B_HEAD):
            col = lo + s * B_HEAD
            yh = y[:, s * B_HEAD:(s + 1) * B_HEAD]
            if col < 2 * qk_w:
                yh = yh * lax.rsqrt(jnp.sum(yh * yh, axis=-1, keepdims=True) + RMS_EPS)
                if col < qk_w:
                    q_ref[0, :, col:col + B_HEAD] = (yh * (B_HEAD ** -0.5)).astype(BF16)
                else:
                    k_ref[0, :, col - qk_w:col - qk_w + B_HEAD] = yh.astype(BF16)
            else:
                v_ref[0, :, col - 2 * qk_w:col - 2 * qk_w + B_HEAD] = yh.astype(BF16)
    z_ref[0] = _dot(h, wz_ref[...]).astype(BF16)
    ba = _dot(h, wba_ref[...])
    beta = jax.nn.sigmoid(ba)
    sp_in = ba + dtb_ref[...]
    softplus = jnp.maximum(sp_in, 0.0) + jnp.log1p(jnp.exp(-jnp.abs(sp_in)))
    g = -jnp.exp(alog_ref[...]) * softplus
    gcum = jnp.dot(lblk_ref[...], g, preferred_element_type=F32, precision=lax.Precision.HIGHEST)
    lane = lax.broadcasted_iota(I32, (tm, 128), 1)
    bg_ref[0] = jnp.where(lane < B_V_HEADS, beta, gcum)


def _gdn_proj(x, g_mix, w_in, conv_w, a_log, dt_bias, tm=256):
    b, t, d = x.shape
    tm = min(tm, t)
    qk_w = B_QK_HEADS * B_HEAD
    v_w = B_V_HEADS * B_HEAD
    cwid = 2 * qk_w + v_w
    wqkv = w_in[:, :cwid].astype(BF16)
    wz = w_in[:, cwid:cwid + v_w].astype(BF16)
    wba = jnp.concatenate([w_in[:, cwid + v_w:], jnp.zeros((d, 128 - 2 * B_V_HEADS), F32)], axis=-1).astype(BF16)
    pad = lambda v: jnp.concatenate([jnp.zeros((B_V_HEADS,), F32), v, jnp.zeros((128 - 2 * B_V_HEADS,), F32)]).reshape(1, 128)
    blk = np.arange(tm) // CHUNK
    lblk = jnp.asarray(((blk[:, None] == blk[None, :]) & (np.arange(tm)[:, None] >= np.arange(tm)[None, :])).astype(np.float32))
    kern = functools.partial(_gdn_proj_kernel, tm=tm, cw=512)
    tok = lambda w: pl.BlockSpec((1, tm, w), lambda i, j: (i, j, 0))
    return pl.pallas_call(
        kern,
        grid=(b, t // tm),
        in_specs=[
            tok(d), _const_spec((1, d)), _const_spec(wqkv.shape), _const_spec(wz.shape), _const_spec(wba.shape),
            _const_spec((B_CONV, cwid)), _const_spec((1, 128)), _const_spec((1, 128)), _const_spec((tm, tm)),
        ],
        out_specs=[tok(qk_w), tok(qk_w), tok(v_w), tok(v_w), tok(128)],
        out_shape=[
            jax.ShapeDtypeStruct((b, t, qk_w), BF16),
            jax.ShapeDtypeStruct((b, t, qk_w), BF16),
            jax.ShapeDtypeStruct((b, t, v_w), BF16),
            jax.ShapeDtypeStruct((b, t, v_w), BF16),
            jax.ShapeDtypeStruct((b, t, 128), F32),
        ],
        scratch_shapes=[pltpu.VMEM((tm + 8, cwid), F32)],
        compiler_params=_cparams("parallel", "arbitrary"),
        name="gdn_proj",
    )(x, g_mix.reshape(1, d), wqkv, wz, wba, conv_w, pad(a_log), pad(dt_bias), lblk)


def _gdn_prep_kernel(q_ref, k_ref, v_ref, bg_ref, u_ref, w_ref, qg_ref, kd_ref, aqk_ref, eg_ref, *, tp):
    hv = pl.program_id(2)
    c = CHUNK
    ri = lax.broadcasted_iota(I32, (c, c), 0)
    ci = lax.broadcasted_iota(I32, (c, c), 1)
    eye = ri == ci
    lane = lax.broadcasted_iota(I32, (c, 128), 1)
    rowi = lax.broadcasted_iota(I32, (c, 1), 0)
    sls = [slice(n * c, (n + 1) * c) for n in range(tp // c)]
    bcols, gcols, decays, invs, xps = [], [], [], [], []
    for n, sl in enumerate(sls):
        bg = bg_ref[0, sl, :]
        bcol = jnp.sum(jnp.where(lane == hv, bg, 0.0), axis=-1, keepdims=True)
        gcol = jnp.sum(jnp.where(lane == hv + B_V_HEADS, bg, 0.0), axis=-1, keepdims=True)
        grow = jnp.sum(jnp.where(eye, gcol, 0.0), axis=0, keepdims=True)
        glast = jnp.sum(jnp.where(rowi == c - 1, gcol, 0.0), axis=0, keepdims=True)
        kb_ = k_ref[0, sl, :]
        qb_ = q_ref[0, sl, :]
        decay = jnp.exp(jnp.where(ri >= ci, gcol - grow, -jnp.inf))
        x_ = jnp.where(ri > ci, -(bcol * _dot_nt(kb_, kb_) * decay), 0.0)
        qg_ref[0, sl, :] = (qb_.astype(F32) * jnp.exp(gcol)).astype(BF16)
        kd_ref[0, sl, :] = (kb_.astype(F32) * jnp.exp(glast - gcol)).astype(BF16)
        aqk = jnp.where(ri >= ci, _dot_nt(qb_, kb_) * decay, 0.0)
        aqk_ref[0, sl, :] = jnp.concatenate([aqk, jnp.zeros_like(aqk)], axis=-1).astype(BF16)
        eg_ref[0, 0, n:n + 1, :] = jnp.broadcast_to(jnp.exp(glast), (1, 128))
        bcols.append(bcol)
        gcols.append(gcol)
        xps.append(x_)
        invs.append(jnp.where(eye, 1.0, 0.0) + x_)
    for _ in range(5):
        xbs = [xp.astype(BF16) for xp in xps]
        xps = [_dot(xb, xb) for xb in xbs]
        invs = [inv + _dot(inv.astype(BF16), xp.astype(BF16)) for inv, xp in zip(invs, xps)]
    for sl, inv, bcol, gcol in zip(sls, invs, bcols, gcols):
        rhs = jnp.concatenate([v_ref[0, sl, :].astype(F32) * bcol,
                               k_ref[0, sl, :].astype(F32) * (bcol * jnp.exp(gcol))], axis=-1)
        uw = _dot(inv.astype(BF16), rhs.astype(BF16))
        u_ref[0, sl, :] = uw[:, :B_HEAD]
        w_ref[0, sl, :] = uw[:, B_HEAD:].astype(BF16)


def _gdn_prep(q, k, v, bg, tp=1024):
    b, t, _ = q.shape
    tp = min(tp, t)
    nv = B_V_HEADS
    rep = B_V_HEADS // B_QK_HEADS
    v_w = nv * B_HEAD
    hspec = lambda: pl.BlockSpec((1, tp, B_HEAD), lambda i, j, h: (i, j, h))
    qkspec = lambda: pl.BlockSpec((1, tp, B_HEAD), lambda i, j, h: (i, j, h // rep))
    return pl.pallas_call(
        functools.partial(_gdn_prep_kernel, tp=tp),
        grid=(b, t // tp, nv),
        in_specs=[qkspec(), qkspec(), hspec(), pl.BlockSpec((1, tp, 128), lambda i, j, h: (i, j, 0))],
        out_specs=[hspec(), hspec(), hspec(), hspec(), hspec(),
                   pl.BlockSpec((1, 1, tp // CHUNK, 128), lambda i, j, h: (i, h, j, 0))],
        out_shape=[
            jax.ShapeDtypeStruct((b, t, v_w), F32),
            jax.ShapeDtypeStruct((b, t, v_w), BF16),
            jax.ShapeDtypeStruct((b, t, v_w), BF16),
            jax.ShapeDtypeStruct((b, t, v_w), BF16),
            jax.ShapeDtypeStruct((b, t, v_w), BF16),
            jax.ShapeDtypeStruct((b, nv, t // CHUNK, 128), F32),
        ],
        compiler_params=_cparams("parallel", "parallel", "parallel"),
        name="gdn_prep",
    )(q, k, v, bg)


def _gdn_scan_kernel(u_ref, w_ref, qg_ref, kd_ref, aqk_ref, eg_ref, z_ref, no_ref, o_ref, state_ref, *, tt, hb):
    @pl.when(pl.program_id(2) == 0)
    def _():
        state_ref[...] = jnp.zeros(state_ref.shape, F32)

    c = CHUNK

    def chunk_body(n, carry):
        r0 = pl.multiple_of(n * c, c)
        rs = pl.ds(r0, c)
        lss = [slice(hh * B_HEAD, (hh + 1) * B_HEAD) for hh in range(hb)]
        sts = [state_ref[hh] for hh in range(hb)]
        stbs = [st.astype(BF16) for st in sts]
        vbs = [(u_ref[0, rs, ls] - _dot(w_ref[0, rs, ls], stb)).astype(BF16) for ls, stb in zip(lss, stbs)]
        os_ = [_dot(qg_ref[0, rs, ls], stb) + _dot(aqk_ref[0, rs, ls][:, 0:c], vb)
               for ls, stb, vb in zip(lss, stbs, vbs)]
        for hh, (ls, st, vb) in enumerate(zip(lss, sts, vbs)):
            eg = eg_ref[0, hh, pl.ds(n, 1), :]
            upd = lax.dot_general(kd_ref[0, rs, ls], vb, (((0,), (0,)), ((), ())), preferred_element_type=F32)
            state_ref[hh] = st * eg + upd
        for ls, o in zip(lss, os_):
            on = o * lax.rsqrt(jnp.mean(o * o, axis=-1, keepdims=True) + RMS_EPS) * no_ref[...]
            o_ref[0, rs, ls] = (on * _silu(z_ref[0, rs, ls].astype(F32))).astype(BF16)
        return carry

    lax.fori_loop(0, tt // c, chunk_body, 0)


def _gdn_scan(u, w, qg, kd, aqk, eg, z, norm_o, tt=512, hb=16):
    b, t, v_w = u.shape
    tt = min(tt, t)
    nv = B_V_HEADS
    wid = hb * B_HEAD
    spec = lambda: pl.BlockSpec((1, tt, wid), lambda i, g, j: (i, j, g))
    return pl.pallas_call(
        functools.partial(_gdn_scan_kernel, tt=tt, hb=hb),
        grid=(b, nv // hb, t // tt),
        in_specs=[spec(), spec(), spec(), spec(), spec(),
                  pl.BlockSpec((1, hb, tt // CHUNK, 128), lambda i, g, j: (i, g, j, 0)),
                  spec(), _const_spec((1, B_HEAD))],
        out_specs=spec(),
        out_shape=jax.ShapeDtypeStruct((b, t, v_w), BF16),
        scratch_shapes=[pltpu.VMEM((hb, B_HEAD, B_HEAD), F32)],
        compiler_params=_cparams("parallel", "parallel", "arbitrary"),
        name="gdn_scan",
    )(u, w, qg, kd, aqk, eg, z, norm_o.reshape(1, B_HEAD))


def _proj_res_kernel(x_ref, a_ref, w_ref, o_ref):
    o_ref[...] = x_ref[...] + _dot(a_ref[...], w_ref[...])


def _proj_res(x2, a2, w, tm=512):
    n, d = x2.shape
    kdim = a2.shape[1]
    tm = min(tm, n)
    return pl.pallas_call(
        _proj_res_kernel,
        grid=(n // tm,),
        in_specs=[pl.BlockSpec((tm, d), lambda i: (i, 0)), pl.BlockSpec((tm, kdim), lambda i: (i, 0)),
                  _const_spec((kdim, d))],
        out_specs=pl.BlockSpec((tm, d), lambda i: (i, 0)),
        out_shape=jax.ShapeDtypeStruct((n, d), F32),
        compiler_params=_cparams("parallel"),
        name="proj_res",
    )(x2, a2, w.astype(BF16))


def _gdn_layer(x, g_mix, w_in, conv_w, a_log, dt_bias, norm_o, w_o):
    b, t, d = x.shape
    q, k, v, z, bg = _gdn_proj(x, g_mix, w_in, conv_w, a_log, dt_bias)
    u, w, qg, kd, aqk, eg = _gdn_prep(q, k, v, bg)
    o = _gdn_scan(u, w, qg, kd, aqk, eg, z, norm_o)
    return _proj_res(x.reshape(b * t, d), o.reshape(b * t, -1), w_o).reshape(b, t, d)


def kernel(x, mem, positions, ffn1_norm, ffn1_w_in, ffn1_w_out, mix_norm, xattn_norm, mem_norm, xattn_w_q, xattn_w_kv, xattn_w_o, ffn2_norm, ffn2_w_in, ffn2_w_out, a_w_in, a_norm_q, a_norm_kv, a_kidx_g, a_kidx_b, a_w_uq, a_w_uk, a_w_uv, a_w_qidx, a_w_o, b_w_in, b_conv, b_a_log, b_dt_bias, b_norm_o, b_w_o, final_norm):
    b, t, d = x.shape
    depth = ffn1_norm.shape[0]
    n_mixers = 2
    for i in range(depth):
        x = _ffn(x.reshape(b * t, d), ffn1_norm[i], ffn1_w_in[i], ffn1_w_out[i]).reshape(b, t, d)
        j = i // n_mixers
        if i % n_mixers == 0:
            x = _dsa_layer(x, positions, mix_norm[i], a_w_in[j], a_norm_q[j], a_norm_kv[j], a_kidx_g[j],
                           a_kidx_b[j], a_w_uq[j], a_w_uk[j], a_w_uv[j], a_w_qidx[j], a_w_o[j])
        else:
            x = _gdn_layer(x, mix_norm[i], b_w_in[j], b_conv[j], b_a_log[j], b_dt_bias[j], b_norm_o[j], b_w_o[j])
        x = _xattn(x, mem, xattn_norm[i], mem_norm[i], xattn_w_q[i], xattn_w_kv[i], xattn_w_o[i])
        last = i == depth - 1
        x = _ffn(x.reshape(b * t, d), ffn2_norm[i], ffn2_w_in[i], ffn2_w_out[i],
                 final_g=final_norm if last else None).reshape(b, t, d)
    return x
```

```python
import functools
import math

import jax
import jax.numpy as jnp
import numpy as np
from jax import lax
from jax.experimental import pallas as pl
from jax.experimental.pallas import tpu as pltpu

F32 = jnp.float32
BF16 = jnp.bfloat16
I32 = jnp.int32

RMS_EPS = 1e-6
ROPE_THETA = 10000.0
CHUNK = 64
TOPK_MAX = 256
A_HEADS = 16
A_LORA = 256
A_NOPE = 64
A_ROPE = 32
A_V = 64
IDX_HEADS = 4
IDX_DIM = 64
KV_W = 384
B_QK_HEADS = 8
B_V_HEADS = 16
B_HEAD = 128
B_CONV = 4
X_HEADS = 4
NEG_BIG = -1e30
INT_MIN = -2147483648
KEY_NEG_INF = -2139095041

VMEM_LIMIT_BYTES = 58 * 1024 * 1024


def _cparams(*sem):
    return pltpu.CompilerParams(dimension_semantics=sem, vmem_limit_bytes=VMEM_LIMIT_BYTES)


def _const_spec(shape):
    nd = len(shape)
    return pl.BlockSpec(shape, lambda *_: (0,) * nd, pipeline_mode=pl.Buffered(1))


def _rms(x, g):
    return x * lax.rsqrt(jnp.mean(x * x, axis=-1, keepdims=True) + RMS_EPS) * g


def _dot(a, b):
    return jnp.dot(a, b, preferred_element_type=F32)


def _dot_nt(a, b):
    return lax.dot_general(a, b, (((1,), (1,)), ((), ())), preferred_element_type=F32)


def _silu(x):
    return x * jax.nn.sigmoid(x)


def _ffn_kernel(x_ref, g_ref, win_ref, wout_ref, fg_ref, o_ref, acc_ref, *, d_ff, f_chunk, final_norm):
    x = x_ref[...]
    h = _rms(x, g_ref[...]).astype(BF16)
    for c in range(d_ff // f_chunk):
        lo = c * f_chunk
        gate = _dot(h, win_ref[:, lo:lo + f_chunk])
        up = _dot(h, win_ref[:, d_ff + lo:d_ff + lo + f_chunk])
        a = (_silu(gate) * up).astype(BF16)
        y = _dot(a, wout_ref[lo:lo + f_chunk, :])
        if c == 0:
            acc_ref[...] = y
        else:
            acc_ref[...] += y
    out = x + 0.5 * acc_ref[...]
    if final_norm:
        out = _rms(out, fg_ref[...])
    o_ref[...] = out


def _ffn(x2, g, w_in, w_out, final_g=None, tm=512):
    n, d = x2.shape
    d_ff = w_out.shape[0]
    f_chunk = 256
    assert n % tm == 0 and d_ff % f_chunk == 0
    fg = final_g if final_g is not None else g
    kern = functools.partial(_ffn_kernel, d_ff=d_ff, f_chunk=f_chunk, final_norm=final_g is not None)
    return pl.pallas_call(
        kern,
        grid=(n // tm,),
        in_specs=[
            pl.BlockSpec((tm, d), lambda i: (i, 0)),
            _const_spec((1, d)),
            _const_spec((d, 2 * d_ff)),
            _const_spec((d_ff, d)),
            _const_spec((1, d)),
        ],
        out_specs=pl.BlockSpec((tm, d), lambda i: (i, 0)),
        out_shape=jax.ShapeDtypeStruct((n, d), F32),
        scratch_shapes=[pltpu.VMEM((tm, d), F32)],
        compiler_params=_cparams("parallel"),
        name="ffn",
    )(x2, g.reshape(1, d), w_in.astype(BF16), w_out.astype(BF16), fg.reshape(1, d))


def _mem_kv_kernel(mem_ref, g_ref, wkv_ref, k_ref, v_ref, *, d):
    m = _rms(mem_ref[0], g_ref[...]).astype(BF16)
    kv = _dot(m, wkv_ref[...])
    k_ref[0] = kv[:, :d].astype(BF16)
    v_ref[0] = kv[:, d:].astype(BF16)


def _xattn_kernel(x_ref, g_ref, wq_ref, k_ref, v_ref, wo_ref, o_ref, *, heads):
    x = x_ref[0]
    d = x.shape[-1]
    hd = d // heads
    h = _rms(x, g_ref[...]).astype(BF16)
    q = (_dot(h, wq_ref[...]) * (hd ** -0.5)).astype(BF16)
    outs = []
    for i in range(heads):
        s = _dot_nt(q[:, i * hd:(i + 1) * hd], k_ref[0, :, i * hd:(i + 1) * hd])
        m = jnp.max(s, axis=-1, keepdims=True)
        p = jnp.exp(s - m)
        l = jnp.sum(p, axis=-1, keepdims=True)
        o = _dot(p.astype(BF16), v_ref[0, :, i * hd:(i + 1) * hd])
        outs.append((o / l).astype(BF16))
    o = jnp.concatenate(outs, axis=-1)
    o_ref[0] = x + _dot(o, wo_ref[...])


def _xattn(x, mem, g_x, g_mem, w_q, w_kv, w_o, tm=512):
    b, t, d = x.shape
    ml = mem.shape[1]
    k, v = pl.pallas_call(
        functools.partial(_mem_kv_kernel, d=d),
        grid=(b,),
        in_specs=[
            pl.BlockSpec((1, ml, d), lambda i: (i, 0, 0)),
            _const_spec((1, d)),
            _const_spec((d, 2 * d)),
        ],
        out_specs=[pl.BlockSpec((1, ml, d), lambda i: (i, 0, 0))] * 2,
        out_shape=[jax.ShapeDtypeStruct((b, ml, d), BF16)] * 2,
        compiler_params=_cparams("parallel"),
        name="mem_kv",
    )(mem, g_mem.reshape(1, d), w_kv.astype(BF16))
    tm = min(tm, t)
    return pl.pallas_call(
        functools.partial(_xattn_kernel, heads=X_HEADS),
        grid=(b, t // tm),
        in_specs=[
            pl.BlockSpec((1, tm, d), lambda i, j: (i, j, 0)),
            _const_spec((1, d)),
            _const_spec((d, d)),
            pl.BlockSpec((1, ml, d), lambda i, j: (i, 0, 0)),
            pl.BlockSpec((1, ml, d), lambda i, j: (i, 0, 0)),
            _const_spec((d, d)),
        ],
        out_specs=pl.BlockSpec((1, tm, d), lambda i, j: (i, j, 0)),
        out_shape=jax.ShapeDtypeStruct((b, t, d), F32),
        compiler_params=_cparams("parallel", "parallel"),
        name="xattn",
    )(x, g_x.reshape(1, d), w_q.astype(BF16), k, v, w_o.astype(BF16))


def _fold_nt_kernel(a_ref, b_ref, o_ref, *, scale):
    o = lax.dot_general(a_ref[0], b_ref[0], (((1,), (1,)), ((), ())),
                        preferred_element_type=F32, precision=lax.Precision.HIGHEST)
    o_ref[...] = (o * scale).astype(o_ref.dtype)


def _fold_nn_kernel(a_ref, b_ref, o_ref):
    o = jnp.dot(a_ref[0], b_ref[0], preferred_element_type=F32, precision=lax.Precision.HIGHEST)
    o_ref[0] = o.astype(o_ref.dtype)


def _fold_weights(w_uq, w_uk, w_uv, w_o, scale):
    nh = A_HEADS
    uq = w_uq.reshape(A_LORA, nh, A_NOPE + A_ROPE)
    uq_nope = jnp.transpose(uq[:, :, :A_NOPE], (1, 0, 2))
    w_qlat = pl.pallas_call(
        functools.partial(_fold_nt_kernel, scale=scale),
        grid=(nh,),
        in_specs=[pl.BlockSpec((1, A_LORA, A_NOPE), lambda h: (h, 0, 0)),
                  pl.BlockSpec((1, A_LORA, A_NOPE), lambda h: (h, 0, 0))],
        out_specs=pl.BlockSpec((A_LORA, A_LORA), lambda h: (0, h)),
        out_shape=jax.ShapeDtypeStruct((A_LORA, nh * A_LORA), BF16),
        compiler_params=_cparams("parallel"),
        name="fold_qlat",
    )(uq_nope, w_uk)
    d = w_o.shape[1]
    w_ov = pl.pallas_call(
        _fold_nn_kernel,
        grid=(nh,),
        in_specs=[pl.BlockSpec((1, A_LORA, A_V), lambda h: (h, 0, 0)),
                  pl.BlockSpec((1, A_V, d), lambda h: (h, 0, 0))],
        out_specs=pl.BlockSpec((1, A_LORA, d), lambda h: (h, 0, 0)),
        out_shape=jax.ShapeDtypeStruct((nh, A_LORA, d), BF16),
        compiler_params=_cparams("parallel"),
        name="fold_ov",
    )(w_uv, w_o.reshape(nh, A_V, d))
    return w_qlat, w_ov


def _dsa_proj_kernel(x_ref, pos_ref, g_ref, wa_ref, nq_ref, nkv_ref, lng_ref, lnb_ref, inv64_ref, inv32_ref,
                     wqi_ref, wpe_ref, wql_ref,
                     kv_ref, kvt_ref, kidx_ref, qidx_ref, widx_ref, q_ref):
    x = x_ref[0]
    tm = x.shape[0]
    h = _rms(x, g_ref[...]).astype(BF16)
    p = _dot(h, wa_ref[...])
    cq = _rms(p[:, 0:256], nq_ref[...])
    ckv = _rms(p[:, 256:512], nkv_ref[...])
    posf = pos_ref[0].astype(F32)
    a64 = posf * inv64_ref[...]
    c64, s64 = jnp.cos(a64), jnp.sin(a64)
    a32 = posf * inv32_ref[...]
    c32, s32 = jnp.cos(a32), jnp.sin(a32)
    lane = lax.broadcasted_iota(I32, (tm, 128), 1)

    g2 = p[:, 512:640]
    kr = g2 * c32 + pltpu.roll(g2, 96, 1) * s32
    kr = jnp.where(lane < A_ROPE, kr, 0.0)
    kv_ref[0, :, 0:256] = ckv.astype(BF16)
    kvt_ref[0, 0] = ckv.T.astype(BF16)
    kv_ref[0, :, 256:384] = kr.astype(BF16)

    g3 = p[:, 640:768]
    valid = lane < IDX_DIM
    mu = jnp.sum(jnp.where(valid, g3, 0.0), axis=-1, keepdims=True) * (1.0 / IDX_DIM)
    dlt = g3 - mu
    var = jnp.sum(jnp.where(valid, dlt * dlt, 0.0), axis=-1, keepdims=True) * (1.0 / IDX_DIM)
    y = dlt * lax.rsqrt(var + RMS_EPS) * lng_ref[...] + lnb_ref[...]
    ki = y * c64 + pltpu.roll(y, 64, 1) * s64
    ki = jnp.where(valid, ki, pltpu.roll(ki, 64, 1))
    ki = ki.astype(BF16)
    kidx_ref[0] = jnp.concatenate([ki, ki], axis=-1)

    g4 = p[:, 768:896] * (IDX_HEADS ** -0.5)
    widx_ref[0] = g4.T[0:8, :]

    cqb = cq.astype(BF16)
    c64x2 = jnp.concatenate([c64, c64], axis=-1)
    s64x2 = jnp.concatenate([s64, s64], axis=-1)
    qi = _dot(cqb, wqi_ref[:, 0:256]) * c64x2 + _dot(cqb, wqi_ref[:, 256:512]) * s64x2
    qidx_ref[0] = qi.astype(BF16)

    c32x4 = jnp.concatenate([c32] * 4, axis=-1)
    s32x4 = jnp.concatenate([s32] * 4, axis=-1)
    qpe = _dot(cqb, wpe_ref[:, 0:512]) * c32x4 + _dot(cqb, wpe_ref[:, 512:1024]) * s32x4
    for hh in range(A_HEADS):
        ql = _dot(cqb, wql_ref[:, hh * 256:(hh + 1) * 256])
        q_ref[0, hh, :, 0:256] = ql.astype(BF16)
        tile = qpe[:, (hh // 4) * 128:(hh // 4 + 1) * 128]
        off = (hh % 4) * A_ROPE
        if off:
            tile = pltpu.roll(tile, 128 - off, 1)
        q_ref[0, hh, :, 256:384] = jnp.where(lane < A_ROPE, tile, 0.0).astype(BF16)


def _rot_half_cols(w, width):
    k = w.shape[0]
    wg = w.reshape(k, -1, width)
    half = A_ROPE // 2
    sw = jnp.concatenate([-wg[:, :, half:A_ROPE], wg[:, :, :half],
                          jnp.zeros((k, wg.shape[1], width - A_ROPE), w.dtype)], axis=-1)
    return sw.reshape(k, -1)


def _dsa_proj(x, positions, g_mix, w_in, norm_q, norm_kv, kidx_g, kidx_b, w_uq, w_qidx, w_qlat, scale, tm=512):
    b, t, d = x.shape
    tm = min(tm, t)
    half = A_ROPE // 2
    o_kr = 2 * A_LORA
    o_ki = o_kr + A_ROPE
    o_w = o_ki + IDX_DIM
    zeros = lambda n: jnp.zeros((d, n), F32)
    w_kr = w_in[:, o_kr:o_ki]
    w_ki = w_in[:, o_ki:o_w]
    w_ki_perm = jnp.concatenate([w_ki[:, half:A_ROPE], w_ki[:, :half]], axis=-1)
    wa = jnp.concatenate([
        w_in[:, :o_kr],
        w_kr, _rot_half_cols(w_kr, A_ROPE), zeros(64),
        w_ki, w_ki_perm, zeros(32),
        w_in[:, o_w:o_w + IDX_HEADS], zeros(128 - IDX_HEADS)], axis=-1).astype(BF16)
    sgn = jnp.concatenate([-jnp.ones((half,), F32), jnp.ones((half,), F32)])
    perm = lambda v: jnp.concatenate([v[half:A_ROPE], v[:half]])
    lng = jnp.concatenate([kidx_g, sgn * perm(kidx_g), jnp.zeros((32,), F32)]).reshape(1, 128)
    lnb = jnp.concatenate([kidx_b, sgn * perm(kidx_b), jnp.zeros((32,), F32)]).reshape(1, 128)
    inv = ROPE_THETA ** (-jnp.arange(0, A_ROPE, 2, dtype=F32) / A_ROPE)
    inv64 = jnp.tile(jnp.concatenate([inv, inv, jnp.zeros((32,), F32)]), 2).reshape(1, 128)
    inv32 = jnp.tile(inv, 8).reshape(1, 128)
    wqi = jnp.concatenate([w_qidx, _rot_half_cols(w_qidx, IDX_DIM)], axis=-1).astype(BF16)
    uq = w_uq.reshape(A_LORA, A_HEADS, A_NOPE + A_ROPE)
    w_pe = (uq[:, :, A_NOPE:] * scale).reshape(A_LORA, A_HEADS * A_ROPE)
    wpe = jnp.concatenate([w_pe, _rot_half_cols(w_pe, A_ROPE)], axis=-1).astype(BF16)
    row = lambda a: a.reshape(1, -1)
    outs = pl.pallas_call(
        _dsa_proj_kernel,
        grid=(b, t // tm),
        in_specs=[
            pl.BlockSpec((1, tm, d), lambda i, j: (i, j, 0)),
            pl.BlockSpec((1, tm, 1), lambda i, j: (i, j, 0)),
            _const_spec((1, d)),
            _const_spec(wa.shape),
            _const_spec((1, A_LORA)), _const_spec((1, A_LORA)),
            _const_spec((1, 128)), _const_spec((1, 128)), _const_spec((1, 128)), _const_spec((1, 128)),
            _const_spec(wqi.shape), _const_spec(wpe.shape), _const_spec(w_qlat.shape),
        ],
        out_specs=[
            pl.BlockSpec((1, tm, KV_W), lambda i, j: (i, j, 0)),
            pl.BlockSpec((1, 1, A_LORA, tm), lambda i, j: (i, j, 0, 0)),
            pl.BlockSpec((1, tm, 256), lambda i, j: (i, j, 0)),
            pl.BlockSpec((1, tm, 256), lambda i, j: (i, j, 0)),
            pl.BlockSpec((1, 8, tm), lambda i, j: (i, 0, j)),
            pl.BlockSpec((1, A_HEADS, tm, KV_W), lambda i, j: (i, 0, j, 0)),
        ],
        out_shape=[
            jax.ShapeDtypeStruct((b, t, KV_W), BF16),
            jax.ShapeDtypeStruct((b, t // tm, A_LORA, tm), BF16),
            jax.ShapeDtypeStruct((b, t, 256), BF16),
            jax.ShapeDtypeStruct((b, t, 256), BF16),
            jax.ShapeDtypeStruct((b, 8, t), F32),
            jax.ShapeDtypeStruct((b, A_HEADS, t, KV_W), BF16),
        ],
        compiler_params=_cparams("parallel", "parallel"),
        name="dsa_proj",
    )(x, positions.reshape(b, t, 1), row(g_mix), wa, row(norm_q), row(norm_kv), lng, lnb, inv64, inv32,
      wqi, wpe, w_qlat)
    return outs


def _dsa_attn_kernel(x_ref, q_ref, qidx_ref, widx_ref, kidx_ref, kv_ref, kvt_ref, wov_ref, ltri_ref,
                     o_ref, keys_ref, m_ref, l_ref, acc_ref, pend_a_ref, pend_p_ref, *, tq, kb, topk):
    i = pl.program_id(1)
    t0 = i * tq
    nkb = (t0 + tq + kb - 1) // kb
    nh = A_HEADS
    rows = nh * tq

    lane_q = lax.broadcasted_iota(I32, (1, tq), 1)
    limit = ((t0 + lane_q) // CHUNK + 1) * CHUNK
    row_k = lax.broadcasted_iota(I32, (kb, tq), 0)

    qi = qidx_ref[0]
    lane256 = lax.broadcasted_iota(I32, (tq, 256), 1)
    qcat = jnp.concatenate([jnp.where((lane256 // IDX_DIM) == hh, qi, jnp.zeros_like(qi))
                            for hh in range(IDX_HEADS)], axis=0)
    wrows = [widx_ref[0, hh:hh + 1, :] for hh in range(IDX_HEADS)]

    def score_block(j, carry):
        kblk = kidx_ref[0, pl.ds(pl.multiple_of(j * kb, kb), kb), :]
        logit = _dot_nt(kblk, qcat)
        sc = jnp.zeros((kb, tq), F32)
        for hh in range(IDX_HEADS):
            sc = sc + jnp.maximum(logit[:, hh * tq:(hh + 1) * tq], 0.0) * wrows[hh]
        sc = sc * (IDX_DIM ** -0.5)
        bits = lax.bitcast_convert_type(sc, I32)
        bits = jnp.where(bits == INT_MIN, 0, bits)
        key = bits ^ ((bits >> 31) & 0x7FFFFFFF)
        adm = (j * kb + row_k) < limit
        keys_ref[j] = jnp.where(adm, key, KEY_NEG_INF)
        return carry

    lax.fori_loop(0, nkb, score_block, 0)

    def count(pred_fn):
        def body(j, c):
            hit = jnp.where(pred_fn(keys_ref[j]), 1, 0).astype(I32)
            return c + jnp.sum(hit.reshape(kb // 8, 8, tq), axis=0)
        c = lax.fori_loop(0, nkb, body, jnp.zeros((8, tq), I32))
        return jnp.sum(c, axis=0, keepdims=True)

    def bit_step(bi, thr):
        cand = thr + jnp.left_shift(jnp.int32(1), 31 - bi)
        cnt = count(lambda k: k >= cand)
        return jnp.where(cnt >= topk, cand, thr)

    thr = lax.fori_loop(0, 32, bit_step, jnp.full((1, tq), INT_MIN, I32))
    need = (topk - count(lambda k: k > thr)).astype(F32)

    m_ref[...] = jnp.full(m_ref.shape, -jnp.inf, F32)
    l_ref[...] = jnp.zeros(l_ref.shape, F32)
    acc_ref[...] = jnp.zeros(acc_ref.shape, F32)
    npair = nh // 2

    hl = A_LORA // 2

    def weighted_values(hp, alpha, pb, kvt):
        acc_ref[hp, 0:hl] = acc_ref[hp, 0:hl] * alpha + _dot(kvt[0:hl], pb)
        acc_ref[hp, hl:] = acc_ref[hp, hl:] * alpha + _dot(kvt[hl:], pb)

    def attn_block(j, tie_carry):
        key = keys_ref[j]
        gt = key > thr
        eq = key == thr
        pref = _dot(ltri_ref[...], jnp.where(eq, 1.0, 0.0).astype(BF16))
        rank = tie_carry + pref
        adm = (j * kb + row_k) < limit
        sel = adm & (gt | (eq & (rank <= need)))
        bias = jnp.where(sel, 0.0, NEG_BIG)
        bias2 = jnp.concatenate([bias, bias], axis=-1)
        kvb = kv_ref[0, pl.ds(pl.multiple_of(j * kb, kb), kb), :]
        kvt = kvt_ref[0, j]
        hk = kb // 2

        def logits(hp):
            qp = q_ref[0, 2 * hp:2 * hp + 2].reshape(2 * tq, KV_W)
            return jnp.concatenate([_dot_nt(kvb[0:hk], qp), _dot_nt(kvb[hk:kb], qp)], axis=0)

        st_next = logits(0)
        pending = (npair - 1, pend_a_ref[0:1, :], pend_p_ref[...], kvt_ref[0, jnp.maximum(j - 1, 0)])
        for hp in range(npair):
            st = st_next + bias2
            if hp + 1 < npair:
                st_next = logits(hp + 1)
            weighted_values(*pending)
            m_prev = m_ref[hp:hp + 1, :]
            m_new = jnp.maximum(m_prev, jnp.max(st, axis=0, keepdims=True))
            alpha = jnp.exp2(m_prev - m_new)
            p = jnp.exp2(st - m_new)
            l_ref[hp:hp + 1, :] = alpha * l_ref[hp:hp + 1, :] + jnp.sum(p, axis=0, keepdims=True)
            m_ref[hp:hp + 1, :] = m_new
            pending = (hp, alpha, p.astype(BF16), kvt)
        pend_a_ref[0:1, :] = pending[1]
        pend_p_ref[...] = pending[2]
        return tie_carry + pref[kb - 1:kb, :]

    pend_a_ref[...] = jnp.ones(pend_a_ref.shape, F32)
    pend_p_ref[...] = jnp.zeros(pend_p_ref.shape, BF16)
    lax.fori_loop(0, nkb, attn_block, jnp.zeros((1, tq), F32))
    weighted_values(npair - 1, pend_a_ref[0:1, :], pend_p_ref[...], kvt_ref[0, nkb - 1])

    y = x_ref[0]
    for hp in range(npair):
        o_pair = (acc_ref[hp] / l_ref[hp:hp + 1, :]).T
        for e in range(2):
            y = y + _dot(o_pair[e * tq:(e + 1) * tq].astype(BF16), wov_ref[2 * hp + e])
    o_ref[0] = y


def _dsa_attn(x, q, qidx, widx, kidx, kv, kvt, w_ov, tq=128):
    b, t, d = x.shape
    kb = kvt.shape[-1]
    topk = min(TOPK_MAX, t // 4)
    tq = min(tq, t)
    assert t % kb == 0 and t % tq == 0 and tq % CHUNK == 0
    ltri = jnp.asarray(np.tril(np.ones((kb, kb), np.float32)), BF16)
    kern = functools.partial(_dsa_attn_kernel, tq=tq, kb=kb, topk=topk)
    per_batch = lambda shape: pl.BlockSpec(shape, lambda i, j: (i,) + (0,) * (len(shape) - 1),
                                           pipeline_mode=pl.Buffered(1))
    return pl.pallas_call(
        kern,
        grid=(b, t // tq),
        in_specs=[
            pl.BlockSpec((1, tq, d), lambda i, j: (i, j, 0)),
            pl.BlockSpec((1, A_HEADS, tq, KV_W), lambda i, j: (i, 0, j, 0)),
            pl.BlockSpec((1, tq, 256), lambda i, j: (i, j, 0)),
            pl.BlockSpec((1, 8, tq), lambda i, j: (i, 0, j)),
            per_batch((1, t, 256)),
            per_batch((1, t, KV_W)),
            per_batch((1, t // kb, A_LORA, kb)),
            _const_spec(w_ov.shape),
            _const_spec((kb, kb)),
        ],
        out_specs=pl.BlockSpec((1, tq, d), lambda i, j: (i, j, 0)),
        out_shape=jax.ShapeDtypeStruct((b, t, d), F32),
        scratch_shapes=[
            pltpu.VMEM((t // kb, kb, tq), I32),
            pltpu.VMEM((A_HEADS // 2, 2 * tq), F32),
            pltpu.VMEM((A_HEADS // 2, 2 * tq), F32),
            pltpu.VMEM((A_HEADS // 2, A_LORA, 2 * tq), F32),
            pltpu.VMEM((8, 2 * tq), F32),
            pltpu.VMEM((kb, 2 * tq), BF16),
        ],
        compiler_params=_cparams("parallel", "parallel"),
        name="dsa_attn",
    )(x, q, qidx, widx, kidx, kv, kvt, w_ov, ltri)


def _dsa_layer(x, positions, g_mix, w_in, norm_q, norm_kv, kidx_g, kidx_b, w_uq, w_uk, w_uv, w_qidx, w_o):
    scale = (A_NOPE + A_ROPE) ** -0.5 * math.log2(math.e)
    w_qlat, w_ov = _fold_weights(w_uq, w_uk, w_uv, w_o, scale)
    kv, kvt, kidx, qidx, widx, q = _dsa_proj(x, positions, g_mix, w_in, norm_q, norm_kv, kidx_g, kidx_b,
                                        w_uq, w_qidx, w_qlat, scale)
    return _dsa_attn(x, q, qidx, widx, kidx, kv, kvt, w_ov)


def _gdn_proj_kernel(x_ref, g_ref, wqkv_ref, wz_ref, wba_ref, conv_ref, alog_ref, dtb_ref, lblk_ref,
                     q_ref, k_ref, v_ref, z_ref, bg_ref, buf_ref, *, tm, cw):
    @pl.when(pl.program_id(1) == 0)
    def _():
        buf_ref[0:8, :] = jnp.zeros((8, buf_ref.shape[1]), F32)

    x = x_ref[0]
    h = _rms(x, g_ref[...]).astype(BF16)
    qk_w = B_QK_HEADS * B_HEAD
    ncol = wqkv_ref.shape[1]
    for c in range(ncol // cw):
        lo = c * cw
        cur = _dot(h, wqkv_ref[:, lo:lo + cw])
        buf_ref[8:8 + tm, lo:lo + cw] = cur
        y = cur * conv_ref[3:4, lo:lo + cw]
        for tap in range(B_CONV - 1):
            sh = B_CONV - 1 - tap
            y = y + buf_ref[8 - sh:8 - sh + tm, lo:lo + cw] * conv_ref[tap:tap + 1, lo:lo + cw]
        buf_ref[0:8, lo:lo + cw] = buf_ref[tm:tm + 8, lo:lo + cw]
        y = _silu(y)
        for s in range(cw // B_HEAD):
            col = lo + s * B_HEAD
            yh = y[:, s * B_HEAD:(s + 1) * B_HEAD]
            if col < 2 * qk_w:
                yh = yh * lax.rsqrt(jnp.sum(yh * yh, axis=-1, keepdims=True) + RMS_EPS)
                if col < qk_w:
                    q_ref[0, :, col:col + B_HEAD] = (yh * (B_HEAD ** -0.5)).astype(BF16)
                else:
                    k_ref[0, :, col - qk_w:col - qk_w + B_HEAD] = yh.astype(BF16)
            else:
                v_ref[0, :, col - 2 * qk_w:col - 2 * qk_w + B_HEAD] = yh.astype(BF16)
    z_ref[0] = _dot(h, wz_ref[...]).astype(BF16)
    ba = _dot(h, wba_ref[...])
    beta = jax.nn.sigmoid(ba)
    sp_in = ba + dtb_ref[...]
    softplus = jnp.maximum(sp_in, 0.0) + jnp.log1p(jnp.exp(-jnp.abs(sp_in)))
    g = -jnp.exp(alog_ref[...]) * softplus
    gcum = jnp.dot(lblk_ref[...], g, preferred_element_type=F32, precision=lax.Precision.HIGHEST)
    lane = lax.broadcasted_iota(I32, (tm, 128), 1)
    bg_ref[0] = jnp.where(lane < B_V_HEADS, beta, gcum)


def _gdn_proj(x, g_mix, w_in, conv_w, a_log, dt_bias, tm=256):
    b, t, d = x.shape
    tm = min(tm, t)
    qk_w = B_QK_HEADS * B_HEAD
    v_w = B_V_HEADS * B_HEAD
    cwid = 2 * qk_w + v_w
    wqkv = w_in[:, :cwid].astype(BF16)
    wz = w_in[:, cwid:cwid + v_w].astype(BF16)
    wba = jnp.concatenate([w_in[:, cwid + v_w:], jnp.zeros((d, 128 - 2 * B_V_HEADS), F32)], axis=-1).astype(BF16)
    pad = lambda v: jnp.concatenate([jnp.zeros((B_V_HEADS,), F32), v, jnp.zeros((128 - 2 * B_V_HEADS,), F32)]).reshape(1, 128)
    blk = np.arange(tm) // CHUNK
    lblk = jnp.asarray(((blk[:, None] == blk[None, :]) & (np.arange(tm)[:, None] >= np.arange(tm)[None, :])).astype(np.float32))
    kern = functools.partial(_gdn_proj_kernel, tm=tm, cw=512)
    tok = lambda w: pl.BlockSpec((1, tm, w), lambda i, j: (i, j, 0))
    return pl.pallas_call(
        kern,
        grid=(b, t // tm),
        in_specs=[
            tok(d), _const_spec((1, d)), _const_spec(wqkv.shape), _const_spec(wz.shape), _const_spec(wba.shape),
            _const_spec((B_CONV, cwid)), _const_spec((1, 128)), _const_spec((1, 128)), _const_spec((tm, tm)),
        ],
        out_specs=[tok(qk_w), tok(qk_w), tok(v_w), tok(v_w), tok(128)],
        out_shape=[
            jax.ShapeDtypeStruct((b, t, qk_w), BF16),
            jax.ShapeDtypeStruct((b, t, qk_w), BF16),
            jax.ShapeDtypeStruct((b, t, v_w), BF16),
            jax.ShapeDtypeStruct((b, t, v_w), BF16),
            jax.ShapeDtypeStruct((b, t, 128), F32),
        ],
        scratch_shapes=[pltpu.VMEM((tm + 8, cwid), F32)],
        compiler_params=_cparams("parallel", "arbitrary"),
        name="gdn_proj",
    )(x, g_mix.reshape(1, d), wqkv, wz, wba, conv_w, pad(a_log), pad(dt_bias), lblk)


def _gdn_prep_kernel(q_ref, k_ref, v_ref, bg_ref, u_ref, w_ref, qg_ref, kd_ref, aqk_ref, eg_ref, *, tp):
    hv = pl.program_id(2)
    c = CHUNK
    ri = lax.broadcasted_iota(I32, (c, c), 0)
    ci = lax.broadcasted_iota(I32, (c, c), 1)
    eye = ri == ci
    lane = lax.broadcasted_iota(I32, (c, 128), 1)
    rowi = lax.broadcasted_iota(I32, (c, 1), 0)
    sls = [slice(n * c, (n + 1) * c) for n in range(tp // c)]
    bcols, gcols, decays, invs, xps = [], [], [], [], []
    for n, sl in enumerate(sls):
        bg = bg_ref[0, sl, :]
        bcol = jnp.sum(jnp.where(lane == hv, bg, 0.0), axis=-1, keepdims=True)
        gcol = jnp.sum(jnp.where(lane == hv + B_V_HEADS, bg, 0.0), axis=-1, keepdims=True)
        grow = jnp.sum(jnp.where(eye, gcol, 0.0), axis=0, keepdims=True)
        glast = jnp.sum(jnp.where(rowi == c - 1, gcol, 0.0), axis=0, keepdims=True)
        kb_ = k_ref[0, sl, :]
        qb_ = q_ref[0, sl, :]
        decay = jnp.exp(jnp.where(ri >= ci, gcol - grow, -jnp.inf))
        x_ = jnp.where(ri > ci, -(bcol * _dot_nt(kb_, kb_) * decay), 0.0)
        qg_ref[0, sl, :] = (qb_.astype(F32) * jnp.exp(gcol)).astype(BF16)
        kd_ref[0, sl, :] = (kb_.astype(F32) * jnp.exp(glast - gcol)).astype(BF16)
        aqk = jnp.where(ri >= ci, _dot_nt(qb_, kb_) * decay, 0.0)
        aqk_ref[0, sl, :] = jnp.concatenate([aqk, jnp.zeros_like(aqk)], axis=-1).astype(BF16)
        eg_ref[0, 0, n:n + 1, :] = jnp.broadcast_to(jnp.exp(glast), (1, 128))
        bcols.append(bcol)
        gcols.append(gcol)
        xps.append(x_)
        invs.append(jnp.where(eye, 1.0, 0.0) + x_)
    for _ in range(5):
        xbs = [xp.astype(BF16) for xp in xps]
        xps = [_dot(xb, xb) for xb in xbs]
        invs = [inv + _dot(inv.astype(BF16), xp.astype(BF16)) for inv, xp in zip(invs, xps)]
    for sl, inv, bcol, gcol in zip(sls, invs, bcols, gcols):
        rhs = jnp.concatenate([v_ref[0, sl, :].astype(F32) * bcol,
                               k_ref[0, sl, :].astype(F32) * (bcol * jnp.exp(gcol))], axis=-1)
        uw = _dot(inv.astype(BF16), rhs.astype(BF16))
        u_ref[0, sl, :] = uw[:, :B_HEAD]
        w_ref[0, sl, :] = uw[:, B_HEAD:].astype(BF16)


def _gdn_prep(q, k, v, bg, tp=2048):
    b, t, _ = q.shape
    tp = min(tp, t)
    nv = B_V_HEADS
    rep = B_V_HEADS // B_QK_HEADS
    v_w = nv * B_HEAD
    hspec = lambda: pl.BlockSpec((1, tp, B_HEAD), lambda i, j, h: (i, j, h))
    qkspec = lambda: pl.BlockSpec((1, tp, B_HEAD), lambda i, j, h: (i, j, h // rep))
    return pl.pallas_call(
        functools.partial(_gdn_prep_kernel, tp=tp),
        grid=(b, t // tp, nv),
        in_specs=[qkspec(), qkspec(), hspec(), pl.BlockSpec((1, tp, 128), lambda i, j, h: (i, j, 0))],
        out_specs=[hspec(), hspec(), hspec(), hspec(), hspec(),
                   pl.BlockSpec((1, 1, tp // CHUNK, 128), lambda i, j, h: (i, h, j, 0))],
        out_shape=[
            jax.ShapeDtypeStruct((b, t, v_w), F32),
            jax.ShapeDtypeStruct((b, t, v_w), BF16),
            jax.ShapeDtypeStruct((b, t, v_w), BF16),
            jax.ShapeDtypeStruct((b, t, v_w), BF16),
            jax.ShapeDtypeStruct((b, t, v_w), BF16),
            jax.ShapeDtypeStruct((b, nv, t // CHUNK, 128), F32),
        ],
        compiler_params=_cparams("parallel", "parallel", "parallel"),
        name="gdn_prep",
    )(q, k, v, bg)


def _gdn_scan_kernel(u_ref, w_ref, qg_ref, kd_ref, aqk_ref, eg_ref, z_ref, no_ref, o_ref, state_ref, *, tt, hb):
    @pl.when(pl.program_id(2) == 0)
    def _():
        state_ref[...] = jnp.zeros(state_ref.shape, F32)

    c = CHUNK

    def chunk_body(n, carry):
        r0 = pl.multiple_of(n * c, c)
        rs = pl.ds(r0, c)
        lss = [slice(hh * B_HEAD, (hh + 1) * B_HEAD) for hh in range(hb)]
        sts = [state_ref[hh] for hh in range(hb)]
        stbs = [st.astype(BF16) for st in sts]
        vbs = [(u_ref[0, rs, ls] - _dot(w_ref[0, rs, ls], stb)).astype(BF16) for ls, stb in zip(lss, stbs)]
        os_ = [_dot(qg_ref[0, rs, ls], stb) + _dot(aqk_ref[0, rs, ls][:, 0:c], vb)
               for ls, stb, vb in zip(lss, stbs, vbs)]
        for hh, (ls, st, vb) in enumerate(zip(lss, sts, vbs)):
            eg = eg_ref[0, hh, pl.ds(n, 1), :]
            upd = lax.dot_general(kd_ref[0, rs, ls], vb, (((0,), (0,)), ((), ())), preferred_element_type=F32)
            state_ref[hh] = st * eg + upd
        for ls, o in zip(lss, os_):
            on = o * lax.rsqrt(jnp.mean(o * o, axis=-1, keepdims=True) + RMS_EPS) * no_ref[...]
            o_ref[0, rs, ls] = (on * _silu(z_ref[0, rs, ls].astype(F32))).astype(BF16)
        return carry

    lax.fori_loop(0, tt // c, chunk_body, 0)


def _gdn_scan(u, w, qg, kd, aqk, eg, z, norm_o, tt=512, hb=16):
    b, t, v_w = u.shape
    tt = min(tt, t)
    nv = B_V_HEADS
    wid = hb * B_HEAD
    spec = lambda: pl.BlockSpec((1, tt, wid), lambda i, g, j: (i, j, g))
    return pl.pallas_call(
        functools.partial(_gdn_scan_kernel, tt=tt, hb=hb),
        grid=(b, nv // hb, t // tt),
        in_specs=[spec(), spec(), spec(), spec(), spec(),
                  pl.BlockSpec((1, hb, tt // CHUNK, 128), lambda i, g, j: (i, g, j, 0)),
                  spec(), _const_spec((1, B_HEAD))],
        out_specs=spec(),
        out_shape=jax.ShapeDtypeStruct((b, t, v_w), BF16),
        scratch_shapes=[pltpu.VMEM((hb, B_HEAD, B_HEAD), F32)],
        compiler_params=_cparams("parallel", "parallel", "arbitrary"),
        name="gdn_scan",
    )(u, w, qg, kd, aqk, eg, z, norm_o.reshape(1, B_HEAD))


def _proj_res_kernel(x_ref, a_ref, w_ref, o_ref):
    o_ref[...] = x_ref[...] + _dot(a_ref[...], w_ref[...])


def _proj_res(x2, a2, w, tm=512):
    n, d = x2.shape
    kdim = a2.shape[1]
    tm = min(tm, n)
    return pl.pallas_call(
        _proj_res_kernel,
        grid=(n // tm,),
        in_specs=[pl.BlockSpec((tm, d), lambda i: (i, 0)), pl.BlockSpec((tm, kdim), lambda i: (i, 0)),
                  _const_spec((kdim, d))],
        out_specs=pl.BlockSpec((tm, d), lambda i: (i, 0)),
        out_shape=jax.ShapeDtypeStruct((n, d), F32),
        compiler_params=_cparams("parallel"),
        name="proj_res",
    )(x2, a2, w.astype(BF16))


def _gdn_layer(x, g_mix, w_in, conv_w, a_log, dt_bias, norm_o, w_o):
    b, t, d = x.shape
    q, k, v, z, bg = _gdn_proj(x, g_mix, w_in, conv_w, a_log, dt_bias)
    u, w, qg, kd, aqk, eg = _gdn_prep(q, k, v, bg)
    o = _gdn_scan(u, w, qg, kd, aqk, eg, z, norm_o)
    return _proj_res(x.reshape(b * t, d), o.reshape(b * t, -1), w_o).reshape(b, t, d)


def kernel(x, mem, positions, ffn1_norm, ffn1_w_in, ffn1_w_out, mix_norm, xattn_norm, mem_norm, xattn_w_q, xattn_w_kv, xattn_w_o, ffn2_norm, ffn2_w_in, ffn2_w_out, a_w_in, a_norm_q, a_norm_kv, a_kidx_g, a_kidx_b, a_w_uq, a_w_uk, a_w_uv, a_w_qidx, a_w_o, b_w_in, b_conv, b_a_log, b_dt_bias, b_norm_o, b_w_o, final_norm):
    b, t, d = x.shape
    depth = ffn1_norm.shape[0]
    n_mixers = 2
    for i in range(depth):
        x = _ffn(x.reshape(b * t, d), ffn1_norm[i], ffn1_w_in[i], ffn1_w_out[i]).reshape(b, t, d)
        j = i // n_mixers
        if i % n_mixers == 0:
            x = _dsa_layer(x, positions, mix_norm[i], a_w_in[j], a_norm_q[j], a_norm_kv[j], a_kidx_g[j],
                           a_kidx_b[j], a_w_uq[j], a_w_uk[j], a_w_uv[j], a_w_qidx[j], a_w_o[j])
        else:
            x = _gdn_layer(x, mix_norm[i], b_w_in[j], b_conv[j], b_a_log[j], b_dt_bias[j], b_norm_o[j], b_w_o[j])
        x = _xattn(x, mem, xattn_norm[i], mem_norm[i], xattn_w_q[i], xattn_w_kv[i], xattn_w_o[i])
        last = i == depth - 1
        x = _ffn(x.reshape(b * t, d), ffn2_norm[i], ffn2_w_in[i], ffn2_w_out[i],
                 final_g=final_norm if last else None).reshape(b, t, d)
    return x
```

```python
import functools
import math

import jax
import jax.numpy as jnp
import numpy as np
from jax import lax
from jax.experimental import pallas as pl
from jax.experimental.pallas import tpu as pltpu

F32 = jnp.float32
BF16 = jnp.bfloat16
I32 = jnp.int32

RMS_EPS = 1e-6
ROPE_THETA = 10000.0
CHUNK = 64
TOPK_MAX = 256
A_HEADS = 16
A_LORA = 256
A_NOPE = 64
A_ROPE = 32
A_V = 64
IDX_HEADS = 4
IDX_DIM = 64
KV_W = 384
B_QK_HEADS = 8
B_V_HEADS = 16
B_HEAD = 128
B_CONV = 4
X_HEADS = 4
NEG_BIG = -1e30
INT_MIN = -2147483648
KEY_NEG_INF = -2139095041

VMEM_LIMIT_BYTES = 58 * 1024 * 1024


def _cparams(*sem):
    return pltpu.CompilerParams(dimension_semantics=sem, vmem_limit_bytes=VMEM_LIMIT_BYTES)


def _const_spec(shape):
    nd = len(shape)
    return pl.BlockSpec(shape, lambda *_: (0,) * nd, pipeline_mode=pl.Buffered(1))


def _rms(x, g):
    return x * lax.rsqrt(jnp.mean(x * x, axis=-1, keepdims=True) + RMS_EPS) * g


def _dot(a, b):
    return jnp.dot(a, b, preferred_element_type=F32)


def _dot_nt(a, b):
    return lax.dot_general(a, b, (((1,), (1,)), ((), ())), preferred_element_type=F32)


def _silu(x):
    return x * jax.nn.sigmoid(x)


def _ffn_kernel(x_ref, g_ref, win_ref, wout_ref, fg_ref, o_ref, acc_ref, *, d_ff, f_chunk, final_norm):
    x = x_ref[...]
    h = _rms(x, g_ref[...]).astype(BF16)
    for c in range(d_ff // f_chunk):
        lo = c * f_chunk
        gate = _dot(h, win_ref[:, lo:lo + f_chunk])
        up = _dot(h, win_ref[:, d_ff + lo:d_ff + lo + f_chunk])
        a = (_silu(gate) * up).astype(BF16)
        y = _dot(a, wout_ref[lo:lo + f_chunk, :])
        if c == 0:
            acc_ref[...] = y
        else:
            acc_ref[...] += y
    out = x + 0.5 * acc_ref[...]
    if final_norm:
        out = _rms(out, fg_ref[...])
    o_ref[...] = out


def _ffn(x2, g, w_in, w_out, final_g=None, tm=512):
    n, d = x2.shape
    d_ff = w_out.shape[0]
    f_chunk = 256
    assert n % tm == 0 and d_ff % f_chunk == 0
    fg = final_g if final_g is not None else g
    kern = functools.partial(_ffn_kernel, d_ff=d_ff, f_chunk=f_chunk, final_norm=final_g is not None)
    return pl.pallas_call(
        kern,
        grid=(n // tm,),
        in_specs=[
            pl.BlockSpec((tm, d), lambda i: (i, 0)),
            _const_spec((1, d)),
            _const_spec((d, 2 * d_ff)),
            _const_spec((d_ff, d)),
            _const_spec((1, d)),
        ],
        out_specs=pl.BlockSpec((tm, d), lambda i: (i, 0)),
        out_shape=jax.ShapeDtypeStruct((n, d), F32),
        scratch_shapes=[pltpu.VMEM((tm, d), F32)],
        compiler_params=_cparams("parallel"),
        name="ffn",
    )(x2, g.reshape(1, d), w_in.astype(BF16), w_out.astype(BF16), fg.reshape(1, d))


def _mem_kv_kernel(mem_ref, g_ref, wkv_ref, k_ref, v_ref, *, d):
    m = _rms(mem_ref[0], g_ref[...]).astype(BF16)
    kv = _dot(m, wkv_ref[...])
    k_ref[0] = kv[:, :d].astype(BF16)
    v_ref[0] = kv[:, d:].astype(BF16)


def _xattn_kernel(x_ref, g_ref, wq_ref, k_ref, v_ref, wo_ref, o_ref, *, heads):
    x = x_ref[0]
    d = x.shape[-1]
    hd = d // heads
    h = _rms(x, g_ref[...]).astype(BF16)
    q = (_dot(h, wq_ref[...]) * (hd ** -0.5)).astype(BF16)
    outs = []
    for i in range(heads):
        s = _dot_nt(q[:, i * hd:(i + 1) * hd], k_ref[0, :, i * hd:(i + 1) * hd])
        m = jnp.max(s, axis=-1, keepdims=True)
        p = jnp.exp(s - m)
        l = jnp.sum(p, axis=-1, keepdims=True)
        o = _dot(p.astype(BF16), v_ref[0, :, i * hd:(i + 1) * hd])
        outs.append((o / l).astype(BF16))
    o = jnp.concatenate(outs, axis=-1)
    o_ref[0] = x + _dot(o, wo_ref[...])


def _xattn(x, mem, g_x, g_mem, w_q, w_kv, w_o, tm=512):
    b, t, d = x.shape
    ml = mem.shape[1]
    k, v = pl.pallas_call(
        functools.partial(_mem_kv_kernel, d=d),
        grid=(b,),
        in_specs=[
            pl.BlockSpec((1, ml, d), lambda i: (i, 0, 0)),
            _const_spec((1, d)),
            _const_spec((d, 2 * d)),
        ],
        out_specs=[pl.BlockSpec((1, ml, d), lambda i: (i, 0, 0))] * 2,
        out_shape=[jax.ShapeDtypeStruct((b, ml, d), BF16)] * 2,
        compiler_params=_cparams("parallel"),
        name="mem_kv",
    )(mem, g_mem.reshape(1, d), w_kv.astype(BF16))
    tm = min(tm, t)
    return pl.pallas_call(
        functools.partial(_xattn_kernel, heads=X_HEADS),
        grid=(b, t // tm),
        in_specs=[
            pl.BlockSpec((1, tm, d), lambda i, j: (i, j, 0)),
            _const_spec((1, d)),
            _const_spec((d, d)),
            pl.BlockSpec((1, ml, d), lambda i, j: (i, 0, 0)),
            pl.BlockSpec((1, ml, d), lambda i, j: (i, 0, 0)),
            _const_spec((d, d)),
        ],
        out_specs=pl.BlockSpec((1, tm, d), lambda i, j: (i, j, 0)),
        out_shape=jax.ShapeDtypeStruct((b, t, d), F32),
        compiler_params=_cparams("parallel", "parallel"),
        name="xattn",
    )(x, g_x.reshape(1, d), w_q.astype(BF16), k, v, w_o.astype(BF16))


def _fold_nt_kernel(a_ref, b_ref, o_ref, *, scale):
    o = lax.dot_general(a_ref[0], b_ref[0], (((1,), (1,)), ((), ())),
                        preferred_element_type=F32, precision=lax.Precision.HIGHEST)
    o_ref[...] = (o * scale).astype(o_ref.dtype)


def _fold_nn_kernel(a_ref, b_ref, o_ref):
    o = jnp.dot(a_ref[0], b_ref[0], preferred_element_type=F32, precision=lax.Precision.HIGHEST)
    o_ref[0] = o.astype(o_ref.dtype)


def _fold_weights(w_uq, w_uk, w_uv, w_o, scale):
    nh = A_HEADS
    uq = w_uq.reshape(A_LORA, nh, A_NOPE + A_ROPE)
    uq_nope = jnp.transpose(uq[:, :, :A_NOPE], (1, 0, 2))
    w_qlat = pl.pallas_call(
        functools.partial(_fold_nt_kernel, scale=scale),
        grid=(nh,),
        in_specs=[pl.BlockSpec((1, A_LORA, A_NOPE), lambda h: (h, 0, 0)),
                  pl.BlockSpec((1, A_LORA, A_NOPE), lambda h: (h, 0, 0))],
        out_specs=pl.BlockSpec((A_LORA, A_LORA), lambda h: (0, h)),
        out_shape=jax.ShapeDtypeStruct((A_LORA, nh * A_LORA), BF16),
        compiler_params=_cparams("parallel"),
        name="fold_qlat",
    )(uq_nope, w_uk)
    d = w_o.shape[1]
    w_ov = pl.pallas_call(
        _fold_nn_kernel,
        grid=(nh,),
        in_specs=[pl.BlockSpec((1, A_LORA, A_V), lambda h: (h, 0, 0)),
                  pl.BlockSpec((1, A_V, d), lambda h: (h, 0, 0))],
        out_specs=pl.BlockSpec((1, A_LORA, d), lambda h: (h, 0, 0)),
        out_shape=jax.ShapeDtypeStruct((nh, A_LORA, d), BF16),
        compiler_params=_cparams("parallel"),
        name="fold_ov",
    )(w_uv, w_o.reshape(nh, A_V, d))
    return w_qlat, w_ov


def _dsa_proj_kernel(x_ref, pos_ref, g_ref, wa_ref, nq_ref, nkv_ref, lng_ref, lnb_ref, inv64_ref, inv32_ref,
                     wqi_ref, wpe_ref, wql_ref,
                     kv_ref, kvt_ref, kidx_ref, qidx_ref, widx_ref, q_ref):
    x = x_ref[0]
    tm = x.shape[0]
    h = _rms(x, g_ref[...]).astype(BF16)
    p = _dot(h, wa_ref[...])
    cq = _rms(p[:, 0:256], nq_ref[...])
    ckv = _rms(p[:, 256:512], nkv_ref[...])
    posf = pos_ref[0].astype(F32)
    a64 = posf * inv64_ref[...]
    c64, s64 = jnp.cos(a64), jnp.sin(a64)
    a32 = posf * inv32_ref[...]
    c32, s32 = jnp.cos(a32), jnp.sin(a32)
    lane = lax.broadcasted_iota(I32, (tm, 128), 1)

    g2 = p[:, 512:640]
    kr = g2 * c32 + pltpu.roll(g2, 96, 1) * s32
    kr = jnp.where(lane < A_ROPE, kr, 0.0)
    kv_ref[0, :, 0:256] = ckv.astype(BF16)
    kvt_ref[0, 0] = ckv.T.astype(BF16)
    kv_ref[0, :, 256:384] = kr.astype(BF16)

    g3 = p[:, 640:768]
    valid = lane < IDX_DIM
    mu = jnp.sum(jnp.where(valid, g3, 0.0), axis=-1, keepdims=True) * (1.0 / IDX_DIM)
    dlt = g3 - mu
    var = jnp.sum(jnp.where(valid, dlt * dlt, 0.0), axis=-1, keepdims=True) * (1.0 / IDX_DIM)
    y = dlt * lax.rsqrt(var + RMS_EPS) * lng_ref[...] + lnb_ref[...]
    ki = y * c64 + pltpu.roll(y, 64, 1) * s64
    ki = jnp.where(valid, ki, pltpu.roll(ki, 64, 1))
    ki = ki.astype(BF16)
    kidx_ref[0] = jnp.concatenate([ki, ki], axis=-1)

    g4 = p[:, 768:896] * (IDX_HEADS ** -0.5)
    widx_ref[0] = g4.T[0:8, :]

    cqb = cq.astype(BF16)
    c64x2 = jnp.concatenate([c64, c64], axis=-1)
    s64x2 = jnp.concatenate([s64, s64], axis=-1)
    qi = _dot(cqb, wqi_ref[:, 0:256]) * c64x2 + _dot(cqb, wqi_ref[:, 256:512]) * s64x2
    qidx_ref[0] = qi.astype(BF16)

    c32x4 = jnp.concatenate([c32] * 4, axis=-1)
    s32x4 = jnp.concatenate([s32] * 4, axis=-1)
    qpe = _dot(cqb, wpe_ref[:, 0:512]) * c32x4 + _dot(cqb, wpe_ref[:, 512:1024]) * s32x4
    for hh in range(A_HEADS):
        ql = _dot(cqb, wql_ref[:, hh * 256:(hh + 1) * 256])
        q_ref[0, hh, :, 0:256] = ql.astype(BF16)
        tile = qpe[:, (hh // 4) * 128:(hh // 4 + 1) * 128]
        off = (hh % 4) * A_ROPE
        if off:
            tile = pltpu.roll(tile, 128 - off, 1)
        q_ref[0, hh, :, 256:384] = jnp.where(lane < A_ROPE, tile, 0.0).astype(BF16)


def _rot_half_cols(w, width):
    k = w.shape[0]
    wg = w.reshape(k, -1, width)
    half = A_ROPE // 2
    sw = jnp.concatenate([-wg[:, :, half:A_ROPE], wg[:, :, :half],
                          jnp.zeros((k, wg.shape[1], width - A_ROPE), w.dtype)], axis=-1)
    return sw.reshape(k, -1)


def _dsa_proj(x, positions, g_mix, w_in, norm_q, norm_kv, kidx_g, kidx_b, w_uq, w_qidx, w_qlat, scale, tm=512):
    b, t, d = x.shape
    tm = min(tm, t)
    half = A_ROPE // 2
    o_kr = 2 * A_LORA
    o_ki = o_kr + A_ROPE
    o_w = o_ki + IDX_DIM
    zeros = lambda n: jnp.zeros((d, n), F32)
    w_kr = w_in[:, o_kr:o_ki]
    w_ki = w_in[:, o_ki:o_w]
    w_ki_perm = jnp.concatenate([w_ki[:, half:A_ROPE], w_ki[:, :half]], axis=-1)
    wa = jnp.concatenate([
        w_in[:, :o_kr],
        w_kr, _rot_half_cols(w_kr, A_ROPE), zeros(64),
        w_ki, w_ki_perm, zeros(32),
        w_in[:, o_w:o_w + IDX_HEADS], zeros(128 - IDX_HEADS)], axis=-1).astype(BF16)
    sgn = jnp.concatenate([-jnp.ones((half,), F32), jnp.ones((half,), F32)])
    perm = lambda v: jnp.concatenate([v[half:A_ROPE], v[:half]])
    lng = jnp.concatenate([kidx_g, sgn * perm(kidx_g), jnp.zeros((32,), F32)]).reshape(1, 128)
    lnb = jnp.concatenate([kidx_b, sgn * perm(kidx_b), jnp.zeros((32,), F32)]).reshape(1, 128)
    inv = ROPE_THETA ** (-jnp.arange(0, A_ROPE, 2, dtype=F32) / A_ROPE)
    inv64 = jnp.tile(jnp.concatenate([inv, inv, jnp.zeros((32,), F32)]), 2).reshape(1, 128)
    inv32 = jnp.tile(inv, 8).reshape(1, 128)
    wqi = jnp.concatenate([w_qidx, _rot_half_cols(w_qidx, IDX_DIM)], axis=-1).astype(BF16)
    uq = w_uq.reshape(A_LORA, A_HEADS, A_NOPE + A_ROPE)
    w_pe = (uq[:, :, A_NOPE:] * scale).reshape(A_LORA, A_HEADS * A_ROPE)
    wpe = jnp.concatenate([w_pe, _rot_half_cols(w_pe, A_ROPE)], axis=-1).astype(BF16)
    row = lambda a: a.reshape(1, -1)
    outs = pl.pallas_call(
        _dsa_proj_kernel,
        grid=(b, t // tm),
        in_specs=[
            pl.BlockSpec((1, tm, d), lambda i, j: (i, j, 0)),
            pl.BlockSpec((1, tm, 1), lambda i, j: (i, j, 0)),
            _const_spec((1, d)),
            _const_spec(wa.shape),
            _const_spec((1, A_LORA)), _const_spec((1, A_LORA)),
            _const_spec((1, 128)), _const_spec((1, 128)), _const_spec((1, 128)), _const_spec((1, 128)),
            _const_spec(wqi.shape), _const_spec(wpe.shape), _const_spec(w_qlat.shape),
        ],
        out_specs=[
            pl.BlockSpec((1, tm, KV_W), lambda i, j: (i, j, 0)),
            pl.BlockSpec((1, 1, A_LORA, tm), lambda i, j: (i, j, 0, 0)),
            pl.BlockSpec((1, tm, 256), lambda i, j: (i, j, 0)),
            pl.BlockSpec((1, tm, 256), lambda i, j: (i, j, 0)),
            pl.BlockSpec((1, 8, tm), lambda i, j: (i, 0, j)),
            pl.BlockSpec((1, A_HEADS, tm, KV_W), lambda i, j: (i, 0, j, 0)),
        ],
        out_shape=[
            jax.ShapeDtypeStruct((b, t, KV_W), BF16),
            jax.ShapeDtypeStruct((b, t // tm, A_LORA, tm), BF16),
            jax.ShapeDtypeStruct((b, t, 256), BF16),
            jax.ShapeDtypeStruct((b, t, 256), BF16),
            jax.ShapeDtypeStruct((b, 8, t), F32),
            jax.ShapeDtypeStruct((b, A_HEADS, t, KV_W), BF16),
        ],
        compiler_params=_cparams("parallel", "parallel"),
        name="dsa_proj",
    )(x, positions.reshape(b, t, 1), row(g_mix), wa, row(norm_q), row(norm_kv), lng, lnb, inv64, inv32,
      wqi, wpe, w_qlat)
    return outs


def _dsa_attn_kernel(x_ref, q_ref, qidx_ref, widx_ref, kidx_ref, kv_ref, kvt_ref, wov_ref, ltri_ref,
                     o_ref, keys_ref, m_ref, l_ref, acc_ref, pend_a_ref, pend_p_ref, *, tq, kb, sb, topk):
    i = pl.program_id(1)
    t0 = i * tq
    nkb = (t0 + tq + kb - 1) // kb
    nh = A_HEADS
    rows = nh * tq

    lane_q = lax.broadcasted_iota(I32, (1, tq), 1)
    limit = ((t0 + lane_q) // CHUNK + 1) * CHUNK
    row_k = lax.broadcasted_iota(I32, (kb, tq), 0)

    qi = qidx_ref[0]
    lane256 = lax.broadcasted_iota(I32, (tq, 256), 1)
    qcat = jnp.concatenate([jnp.where((lane256 // IDX_DIM) == hh, qi, jnp.zeros_like(qi))
                            for hh in range(IDX_HEADS)], axis=0)
    wrows = [widx_ref[0, hh:hh + 1, :] for hh in range(IDX_HEADS)]

    def score_block(j, carry):
        kblk = kidx_ref[0, pl.ds(pl.multiple_of(j * kb, kb), kb), :]
        logit = _dot_nt(kblk, qcat)
        sc = jnp.zeros((kb, tq), F32)
        for hh in range(IDX_HEADS):
            sc = sc + jnp.maximum(logit[:, hh * tq:(hh + 1) * tq], 0.0) * wrows[hh]
        sc = sc * (IDX_DIM ** -0.5)
        bits = lax.bitcast_convert_type(sc, I32)
        bits = jnp.where(bits == INT_MIN, 0, bits)
        key = bits ^ ((bits >> 31) & 0x7FFFFFFF)
        adm = (j * kb + row_k) < limit
        keys_ref[j] = jnp.where(adm, key, KEY_NEG_INF)
        return carry

    lax.fori_loop(0, nkb, score_block, 0)

    def count(pred_fn):
        def body(j, c):
            hit = jnp.where(pred_fn(keys_ref[j]), 1, 0).astype(I32)
            return c + jnp.sum(hit.reshape(kb // 8, 8, tq), axis=0)
        c = lax.fori_loop(0, nkb, body, jnp.zeros((8, tq), I32))
        return jnp.sum(c, axis=0, keepdims=True)

    def bit_step(bi, thr):
        cand = thr + jnp.left_shift(jnp.int32(1), 31 - bi)
        cnt = count(lambda k: k >= cand)
        return jnp.where(cnt >= topk, cand, thr)

    thr = lax.fori_loop(0, 32, bit_step, jnp.full((1, tq), INT_MIN, I32))
    need = (topk - count(lambda k: k > thr)).astype(F32)

    m_ref[...] = jnp.full(m_ref.shape, -jnp.inf, F32)
    l_ref[...] = jnp.zeros(l_ref.shape, F32)
    acc_ref[...] = jnp.zeros(acc_ref.shape, F32)
    npair = nh // 2

    hl = A_LORA // 2

    def weighted_values(hp, alpha, pb, kvt):
        acc_ref[hp, 0:hl] = acc_ref[hp, 0:hl] * alpha + _dot(kvt[0:hl], pb)
        acc_ref[hp, hl:] = acc_ref[hp, hl:] * alpha + _dot(kvt[hl:], pb)

    def attn_block(j, tie_carry):
        key = keys_ref[j]
        gt = key > thr
        eq = key == thr
        eqb = jnp.where(eq, 1.0, 0.0).astype(BF16)
        ranks = []
        run = tie_carry
        for s0 in range(0, kb, sb):
            ranks.append(_dot(ltri_ref[...], eqb[s0:s0 + sb]) + run)
            run = ranks[-1][sb - 1:sb, :]
        rank = jnp.concatenate(ranks, axis=0)
        adm = (j * kb + row_k) < limit
        sel = adm & (gt | (eq & (rank <= need)))
        bias = jnp.where(sel, 0.0, NEG_BIG)
        bias2 = jnp.concatenate([bias, bias], axis=-1)
        kvb = kv_ref[0, pl.ds(pl.multiple_of(j * kb, kb), kb), :]
        kvt = kvt_ref[0, j]
        hk = kb // 2

        def logits(hp):
            qp = q_ref[0, 2 * hp:2 * hp + 2].reshape(2 * tq, KV_W)
            return jnp.concatenate([_dot_nt(kvb[0:hk], qp), _dot_nt(kvb[hk:kb], qp)], axis=0)

        st_next = logits(0)
        pending = (npair - 1, pend_a_ref[0:1, :], pend_p_ref[...], kvt_ref[0, jnp.maximum(j - 1, 0)])
        for hp in range(npair):
            st = st_next + bias2
            if hp + 1 < npair:
                st_next = logits(hp + 1)
            weighted_values(*pending)
            m_prev = m_ref[hp:hp + 1, :]
            m_new = jnp.maximum(m_prev, jnp.max(st, axis=0, keepdims=True))
            alpha = jnp.exp2(m_prev - m_new)
            p = jnp.exp2(st - m_new)
            l_ref[hp:hp + 1, :] = alpha * l_ref[hp:hp + 1, :] + jnp.sum(p, axis=0, keepdims=True)
            m_ref[hp:hp + 1, :] = m_new
            pending = (hp, alpha, p.astype(BF16), kvt)
        pend_a_ref[0:1, :] = pending[1]
        pend_p_ref[...] = pending[2]
        return run

    pend_a_ref[...] = jnp.ones(pend_a_ref.shape, F32)
    pend_p_ref[...] = jnp.zeros(pend_p_ref.shape, BF16)
    lax.fori_loop(0, nkb, attn_block, jnp.zeros((1, tq), F32))
    weighted_values(npair - 1, pend_a_ref[0:1, :], pend_p_ref[...], kvt_ref[0, nkb - 1])

    y = x_ref[0]
    for hp in range(npair):
        o_pair = (acc_ref[hp] / l_ref[hp:hp + 1, :]).T
        for e in range(2):
            y = y + _dot(o_pair[e * tq:(e + 1) * tq].astype(BF16), wov_ref[2 * hp + e])
    o_ref[0] = y


def _dsa_attn(x, q, qidx, widx, kidx, kv, kvt, w_ov, tq=128):
    b, t, d = x.shape
    kb = kvt.shape[-1]
    topk = min(TOPK_MAX, t // 4)
    tq = min(tq, t)
    assert t % kb == 0 and t % tq == 0 and tq % CHUNK == 0
    sb = min(128, kb)
    ltri = jnp.asarray(np.tril(np.ones((sb, sb), np.float32)), BF16)
    kern = functools.partial(_dsa_attn_kernel, tq=tq, kb=kb, sb=sb, topk=topk)
    per_batch = lambda shape: pl.BlockSpec(shape, lambda i, j: (i,) + (0,) * (len(shape) - 1),
                                           pipeline_mode=pl.Buffered(1))
    return pl.pallas_call(
        kern,
        grid=(b, t // tq),
        in_specs=[
            pl.BlockSpec((1, tq, d), lambda i, j: (i, j, 0)),
            pl.BlockSpec((1, A_HEADS, tq, KV_W), lambda i, j: (i, 0, j, 0)),
            pl.BlockSpec((1, tq, 256), lambda i, j: (i, j, 0)),
            pl.BlockSpec((1, 8, tq), lambda i, j: (i, 0, j)),
            per_batch((1, t, 256)),
            per_batch((1, t, KV_W)),
            per_batch((1, t // kb, A_LORA, kb)),
            _const_spec(w_ov.shape),
            _const_spec((sb, sb)),
        ],
        out_specs=pl.BlockSpec((1, tq, d), lambda i, j: (i, j, 0)),
        out_shape=jax.ShapeDtypeStruct((b, t, d), F32),
        scratch_shapes=[
            pltpu.VMEM((t // kb, kb, tq), I32),
            pltpu.VMEM((A_HEADS // 2, 2 * tq), F32),
            pltpu.VMEM((A_HEADS // 2, 2 * tq), F32),
            pltpu.VMEM((A_HEADS // 2, A_LORA, 2 * tq), F32),
            pltpu.VMEM((8, 2 * tq), F32),
            pltpu.VMEM((kb, 2 * tq), BF16),
        ],
        compiler_params=_cparams("parallel", "parallel"),
        name="dsa_attn",
    )(x, q, qidx, widx, kidx, kv, kvt, w_ov, ltri)


def _dsa_layer(x, positions, g_mix, w_in, norm_q, norm_kv, kidx_g, kidx_b, w_uq, w_uk, w_uv, w_qidx, w_o):
    scale = (A_NOPE + A_ROPE) ** -0.5 * math.log2(math.e)
    w_qlat, w_ov = _fold_weights(w_uq, w_uk, w_uv, w_o, scale)
    kv, kvt, kidx, qidx, widx, q = _dsa_proj(x, positions, g_mix, w_in, norm_q, norm_kv, kidx_g, kidx_b,
                                        w_uq, w_qidx, w_qlat, scale)
    return _dsa_attn(x, q, qidx, widx, kidx, kv, kvt, w_ov)


def _gdn_proj_kernel(x_ref, g_ref, wqkv_ref, wz_ref, wba_ref, conv_ref, alog_ref, dtb_ref, lblk_ref,
                     q_ref, k_ref, v_ref, z_ref, bg_ref, buf_ref, *, tm, cw):
    @pl.when(pl.program_id(1) == 0)
    def _():
        buf_ref[0:8, :] = jnp.zeros((8, buf_ref.shape[1]), F32)

    x = x_ref[0]
    h = _rms(x, g_ref[...]).astype(BF16)
    qk_w = B_QK_HEADS * B_HEAD
    ncol = wqkv_ref.shape[1]
    for c in range(ncol // cw):
        lo = c * cw
        cur = _dot(h, wqkv_ref[:, lo:lo + cw])
        buf_ref[8:8 + tm, lo:lo + cw] = cur
        y = cur * conv_ref[3:4, lo:lo + cw]
        for tap in range(B_CONV - 1):
            sh = B_CONV - 1 - tap
            y = y + buf_ref[8 - sh:8 - sh + tm, lo:lo + cw] * conv_ref[tap:tap + 1, lo:lo + cw]
        buf_ref[0:8, lo:lo + cw] = buf_ref[tm:tm + 8, lo:lo + cw]
        y = _silu(y)
        for s in range(cw // B_HEAD):
            col = lo + s * B_HEAD
            yh = y[:, s * B_HEAD:(s + 1) * B_HEAD]
            if col < 2 * qk_w:
                yh = yh * lax.rsqrt(jnp.sum(yh * yh, axis=-1, keepdims=True) + RMS_EPS)
                if col < qk_w:
                    q_ref[0, :, col:col + B_HEAD] = (yh * (B_HEAD ** -0.5)).astype(BF16)
                else:
                    k_ref[0, :, col - qk_w:col - qk_w + B_HEAD] = yh.astype(BF16)
            else:
                v_ref[0, :, col - 2 * qk_w:col - 2 * qk_w + B_HEAD] = yh.astype(BF16)
    z_ref[0] = _dot(h, wz_ref[...]).astype(BF16)
    ba = _dot(h, wba_ref[...])
    beta = jax.nn.sigmoid(ba)
    sp_in = ba + dtb_ref[...]
    softplus = jnp.maximum(sp_in, 0.0) + jnp.log1p(jnp.exp(-jnp.abs(sp_in)))
    g = -jnp.exp(alog_ref[...]) * softplus
    gcum = jnp.dot(lblk_ref[...], g, preferred_element_type=F32, precision=lax.Precision.HIGHEST)
    lane = lax.broadcasted_iota(I32, (tm, 128), 1)
    bg_ref[0] = jnp.where(lane < B_V_HEADS, beta, gcum)


def _gdn_proj(x, g_mix, w_in, conv_w, a_log, dt_bias, tm=256):
    b, t, d = x.shape
    tm = min(tm, t)
    qk_w = B_QK_HEADS * B_HEAD
    v_w = B_V_HEADS * B_HEAD
    cwid = 2 * qk_w + v_w
    wqkv = w_in[:, :cwid].astype(BF16)
    wz = w_in[:, cwid:cwid + v_w].astype(BF16)
    wba = jnp.concatenate([w_in[:, cwid + v_w:], jnp.zeros((d, 128 - 2 * B_V_HEADS), F32)], axis=-1).astype(BF16)
    pad = lambda v: jnp.concatenate([jnp.zeros((B_V_HEADS,), F32), v, jnp.zeros((128 - 2 * B_V_HEADS,), F32)]).reshape(1, 128)
    blk = np.arange(tm) // CHUNK
    lblk = jnp.asarray(((blk[:, None] == blk[None, :]) & (np.arange(tm)[:, None] >= np.arange(tm)[None, :])).astype(np.float32))
    kern = functools.partial(_gdn_proj_kernel, tm=tm, cw=512)
    tok = lambda w: pl.BlockSpec((1, tm, w), lambda i, j: (i, j, 0))
    return pl.pallas_call(
        kern,
        grid=(b, t // tm),
        in_specs=[
            tok(d), _const_spec((1, d)), _const_spec(wqkv.shape), _const_spec(wz.shape), _const_spec(wba.shape),
            _const_spec((B_CONV, cwid)), _const_spec((1, 128)), _const_spec((1, 128)), _const_spec((tm, tm)),
        ],
        out_specs=[tok(qk_w), tok(qk_w), tok(v_w), tok(v_w), tok(128)],
        out_shape=[
            jax.ShapeDtypeStruct((b, t, qk_w), BF16),
            jax.ShapeDtypeStruct((b, t, qk_w), BF16),
            jax.ShapeDtypeStruct((b, t, v_w), BF16),
            jax.ShapeDtypeStruct((b, t, v_w), BF16),
            jax.ShapeDtypeStruct((b, t, 128), F32),
        ],
        scratch_shapes=[pltpu.VMEM((tm + 8, cwid), F32)],
        compiler_params=_cparams("parallel", "arbitrary"),
        name="gdn_proj",
    )(x, g_mix.reshape(1, d), wqkv, wz, wba, conv_w, pad(a_log), pad(dt_bias), lblk)


def _gdn_prep_kernel(q_ref, k_ref, v_ref, bg_ref, u_ref, w_ref, qg_ref, kd_ref, aqk_ref, eg_ref, *, tp):
    hv = pl.program_id(2)
    c = CHUNK
    ri = lax.broadcasted_iota(I32, (c, c), 0)
    ci = lax.broadcasted_iota(I32, (c, c), 1)
    eye = ri == ci
    lane = lax.broadcasted_iota(I32, (c, 128), 1)
    rowi = lax.broadcasted_iota(I32, (c, 1), 0)
    sls = [slice(n * c, (n + 1) * c) for n in range(tp // c)]
    bcols, gcols, decays, invs, xps = [], [], [], [], []
    for n, sl in enumerate(sls):
        bg = bg_ref[0, sl, :]
        bcol = jnp.sum(jnp.where(lane == hv, bg, 0.0), axis=-1, keepdims=True)
        gcol = jnp.sum(jnp.where(lane == hv + B_V_HEADS, bg, 0.0), axis=-1, keepdims=True)
        grow = jnp.sum(jnp.where(eye, gcol, 0.0), axis=0, keepdims=True)
        glast = jnp.sum(jnp.where(rowi == c - 1, gcol, 0.0), axis=0, keepdims=True)
        kb_ = k_ref[0, sl, :]
        qb_ = q_ref[0, sl, :]
        decay = jnp.exp(jnp.where(ri >= ci, gcol - grow, -jnp.inf))
        x_ = jnp.where(ri > ci, -(bcol * _dot_nt(kb_, kb_) * decay), 0.0)
        qg_ref[0, sl, :] = (qb_.astype(F32) * jnp.exp(gcol)).astype(BF16)
        kd_ref[0, sl, :] = (kb_.astype(F32) * jnp.exp(glast - gcol)).astype(BF16)
        aqk = jnp.where(ri >= ci, _dot_nt(qb_, kb_) * decay, 0.0)
        aqk_ref[0, sl, :] = jnp.concatenate([aqk, jnp.zeros_like(aqk)], axis=-1).astype(BF16)
        eg_ref[0, 0, n:n + 1, :] = jnp.broadcast_to(jnp.exp(glast), (1, 128))
        bcols.append(bcol)
        gcols.append(gcol)
        xps.append(x_)
        invs.append(jnp.where(eye, 1.0, 0.0) + x_)
    for _ in range(5):
        xbs = [xp.astype(BF16) for xp in xps]
        xps = [_dot(xb, xb) for xb in xbs]
        invs = [inv + _dot(inv.astype(BF16), xp.astype(BF16)) for inv, xp in zip(invs, xps)]
    for sl, inv, bcol, gcol in zip(sls, invs, bcols, gcols):
        rhs = jnp.concatenate([v_ref[0, sl, :].astype(F32) * bcol,
                               k_ref[0, sl, :].astype(F32) * (bcol * jnp.exp(gcol))], axis=-1)
        uw = _dot(inv.astype(BF16), rhs.astype(BF16))
        u_ref[0, sl, :] = uw[:, :B_HEAD]
        w_ref[0, sl, :] = uw[:, B_HEAD:].astype(BF16)


def _gdn_prep(q, k, v, bg, tp=2048):
    b, t, _ = q.shape
    tp = min(tp, t)
    nv = B_V_HEADS
    rep = B_V_HEADS // B_QK_HEADS
    v_w = nv * B_HEAD
    hspec = lambda: pl.BlockSpec((1, tp, B_HEAD), lambda i, j, h: (i, j, h))
    qkspec = lambda: pl.BlockSpec((1, tp, B_HEAD), lambda i, j, h: (i, j, h // rep))
    return pl.pallas_call(
        functools.partial(_gdn_prep_kernel, tp=tp),
        grid=(b, t // tp, nv),
        in_specs=[qkspec(), qkspec(), hspec(), pl.BlockSpec((1, tp, 128), lambda i, j, h: (i, j, 0))],
        out_specs=[hspec(), hspec(), hspec(), hspec(), hspec(),
                   pl.BlockSpec((1, 1, tp // CHUNK, 128), lambda i, j, h: (i, h, j, 0))],
        out_shape=[
            jax.ShapeDtypeStruct((b, t, v_w), F32),
            jax.ShapeDtypeStruct((b, t, v_w), BF16),
            jax.ShapeDtypeStruct((b, t, v_w), BF16),
            jax.ShapeDtypeStruct((b, t, v_w), BF16),
            jax.ShapeDtypeStruct((b, t, v_w), BF16),
            jax.ShapeDtypeStruct((b, nv, t // CHUNK, 128), F32),
        ],
        compiler_params=_cparams("parallel", "parallel", "parallel"),
        name="gdn_prep",
    )(q, k, v, bg)


def _gdn_scan_kernel(u_ref, w_ref, qg_ref, kd_ref, aqk_ref, eg_ref, z_ref, no_ref, o_ref, state_ref, *, tt, hb):
    @pl.when(pl.program_id(2) == 0)
    def _():
        state_ref[...] = jnp.zeros(state_ref.shape, F32)

    c = CHUNK

    def chunk_body(n, carry):
        r0 = pl.multiple_of(n * c, c)
        rs = pl.ds(r0, c)
        lss = [slice(hh * B_HEAD, (hh + 1) * B_HEAD) for hh in range(hb)]
        sts = [state_ref[hh] for hh in range(hb)]
        stbs = [st.astype(BF16) for st in sts]
        vbs = [(u_ref[0, rs, ls] - _dot(w_ref[0, rs, ls], stb)).astype(BF16) for ls, stb in zip(lss, stbs)]
        os_ = [_dot(qg_ref[0, rs, ls], stb) + _dot(aqk_ref[0, rs, ls][:, 0:c], vb)
               for ls, stb, vb in zip(lss, stbs, vbs)]
        for hh, (ls, st, vb) in enumerate(zip(lss, sts, vbs)):
            eg = eg_ref[0, hh, pl.ds(n, 1), :]
            upd = lax.dot_general(kd_ref[0, rs, ls], vb, (((0,), (0,)), ((), ())), preferred_element_type=F32)
            state_ref[hh] = st * eg + upd
        for ls, o in zip(lss, os_):
            on = o * lax.rsqrt(jnp.mean(o * o, axis=-1, keepdims=True) + RMS_EPS) * no_ref[...]
            o_ref[0, rs, ls] = (on * _silu(z_ref[0, rs, ls].astype(F32))).astype(BF16)
        return carry

    lax.fori_loop(0, tt // c, chunk_body, 0)


def _gdn_scan(u, w, qg, kd, aqk, eg, z, norm_o, tt=512, hb=16):
    b, t, v_w = u.shape
    tt = min(tt, t)
    nv = B_V_HEADS
    wid = hb * B_HEAD
    spec = lambda: pl.BlockSpec((1, tt, wid), lambda i, g, j: (i, j, g))
    return pl.pallas_call(
        functools.partial(_gdn_scan_kernel, tt=tt, hb=hb),
        grid=(b, nv // hb, t // tt),
        in_specs=[spec(), spec(), spec(), spec(), spec(),
                  pl.BlockSpec((1, hb, tt // CHUNK, 128), lambda i, g, j: (i, g, j, 0)),
                  spec(), _const_spec((1, B_HEAD))],
        out_specs=spec(),
        out_shape=jax.ShapeDtypeStruct((b, t, v_w), BF16),
        scratch_shapes=[pltpu.VMEM((hb, B_HEAD, B_HEAD), F32)],
        compiler_params=_cparams("parallel", "parallel", "arbitrary"),
        name="gdn_scan",
    )(u, w, qg, kd, aqk, eg, z, norm_o.reshape(1, B_HEAD))


def _proj_res_kernel(x_ref, a_ref, w_ref, o_ref):
    o_ref[...] = x_ref[...] + _dot(a_ref[...], w_ref[...])


def _proj_res(x2, a2, w, tm=512):
    n, d = x2.shape
    kdim = a2.shape[1]
    tm = min(tm, n)
    return pl.pallas_call(
        _proj_res_kernel,
        grid=(n // tm,),
        in_specs=[pl.BlockSpec((tm, d), lambda i: (i, 0)), pl.BlockSpec((tm, kdim), lambda i: (i, 0)),
                  _const_spec((kdim, d))],
        out_specs=pl.BlockSpec((tm, d), lambda i: (i, 0)),
        out_shape=jax.ShapeDtypeStruct((n, d), F32),
        compiler_params=_cparams("parallel"),
        name="proj_res",
    )(x2, a2, w.astype(BF16))


def _gdn_layer(x, g_mix, w_in, conv_w, a_log, dt_bias, norm_o, w_o):
    b, t, d = x.shape
    q, k, v, z, bg = _gdn_proj(x, g_mix, w_in, conv_w, a_log, dt_bias)
    u, w, qg, kd, aqk, eg = _gdn_prep(q, k, v, bg)
    o = _gdn_scan(u, w, qg, kd, aqk, eg, z, norm_o)
    return _proj_res(x.reshape(b * t, d), o.reshape(b * t, -1), w_o).reshape(b, t, d)


def kernel(x, mem, positions, ffn1_norm, ffn1_w_in, ffn1_w_out, mix_norm, xattn_norm, mem_norm, xattn_w_q, xattn_w_kv, xattn_w_o, ffn2_norm, ffn2_w_in, ffn2_w_out, a_w_in, a_norm_q, a_norm_kv, a_kidx_g, a_kidx_b, a_w_uq, a_w_uk, a_w_uv, a_w_qidx, a_w_o, b_w_in, b_conv, b_a_log, b_dt_bias, b_norm_o, b_w_o, final_norm):
    b, t, d = x.shape
    depth = ffn1_norm.shape[0]
    n_mixers = 2
    for i in range(depth):
        x = _ffn(x.reshape(b * t, d), ffn1_norm[i], ffn1_w_in[i], ffn1_w_out[i]).reshape(b, t, d)
        j = i // n_mixers
        if i % n_mixers == 0:
            x = _dsa_layer(x, positions, mix_norm[i], a_w_in[j], a_norm_q[j], a_norm_kv[j], a_kidx_g[j],
                           a_kidx_b[j], a_w_uq[j], a_w_uk[j], a_w_uv[j], a_w_qidx[j], a_w_o[j])
        else:
            x = _gdn_layer(x, mix_norm[i], b_w_in[j], b_conv[j], b_a_log[j], b_dt_bias[j], b_norm_o[j], b_w_o[j])
        x = _xattn(x, mem, xattn_norm[i], mem_norm[i], xattn_w_q[i], xattn_w_kv[i], xattn_w_o[i])
        last = i == depth - 1
        x = _ffn(x.reshape(b * t, d), ffn2_norm[i], ffn2_w_in[i], ffn2_w_out[i],
                 final_g=final_norm if last else None).reshape(b, t, d)
    return x
```

```python
import functools
import math

import jax
import jax.numpy as jnp
import numpy as np
from jax import lax
from jax.experimental import pallas as pl
from jax.experimental.pallas import tpu as pltpu

F32 = jnp.float32
BF16 = jnp.bfloat16
I32 = jnp.int32

RMS_EPS = 1e-6
ROPE_THETA = 10000.0
CHUNK = 64
TOPK_MAX = 256
A_HEADS = 16
A_LORA = 256
A_NOPE = 64
A_ROPE = 32
A_V = 64
IDX_HEADS = 4
IDX_DIM = 64
KV_W = 384
B_QK_HEADS = 8
B_V_HEADS = 16
B_HEAD = 128
B_CONV = 4
X_HEADS = 4
NEG_BIG = -1e30
INT_MIN = -2147483648
KEY_NEG_INF = -2139095041

VMEM_LIMIT_BYTES = 58 * 1024 * 1024


def _cparams(*sem):
    return pltpu.CompilerParams(dimension_semantics=sem, vmem_limit_bytes=VMEM_LIMIT_BYTES)


def _const_spec(shape):
    nd = len(shape)
    return pl.BlockSpec(shape, lambda *_: (0,) * nd, pipeline_mode=pl.Buffered(1))


def _rms(x, g):
    return x * lax.rsqrt(jnp.mean(x * x, axis=-1, keepdims=True) + RMS_EPS) * g


def _dot(a, b):
    return jnp.dot(a, b, preferred_element_type=F32)


def _dot_nt(a, b):
    return lax.dot_general(a, b, (((1,), (1,)), ((), ())), preferred_element_type=F32)


def _silu(x):
    return x * jax.nn.sigmoid(x)


def _ffn_kernel(x_ref, g_ref, win_ref, wout_ref, fg_ref, o_ref, acc_ref, *, d_ff, f_chunk, final_norm):
    x = x_ref[...]
    h = _rms(x, g_ref[...]).astype(BF16)
    for c in range(d_ff // f_chunk):
        lo = c * f_chunk
        gate = _dot(h, win_ref[:, lo:lo + f_chunk])
        up = _dot(h, win_ref[:, d_ff + lo:d_ff + lo + f_chunk])
        a = (_silu(gate) * up).astype(BF16)
        y = _dot(a, wout_ref[lo:lo + f_chunk, :])
        if c == 0:
            acc_ref[...] = y
        else:
            acc_ref[...] += y
    out = x + 0.5 * acc_ref[...]
    if final_norm:
        out = _rms(out, fg_ref[...])
    o_ref[...] = out


def _ffn(x2, g, w_in, w_out, final_g=None, tm=512):
    n, d = x2.shape
    d_ff = w_out.shape[0]
    f_chunk = 256
    assert n % tm == 0 and d_ff % f_chunk == 0
    fg = final_g if final_g is not None else g
    kern = functools.partial(_ffn_kernel, d_ff=d_ff, f_chunk=f_chunk, final_norm=final_g is not None)
    return pl.pallas_call(
        kern,
        grid=(n // tm,),
        in_specs=[
            pl.BlockSpec((tm, d), lambda i: (i, 0)),
            _const_spec((1, d)),
            _const_spec((d, 2 * d_ff)),
            _const_spec((d_ff, d)),
            _const_spec((1, d)),
        ],
        out_specs=pl.BlockSpec((tm, d), lambda i: (i, 0)),
        out_shape=jax.ShapeDtypeStruct((n, d), F32),
        scratch_shapes=[pltpu.VMEM((tm, d), F32)],
        compiler_params=_cparams("parallel"),
        name="ffn",
    )(x2, g.reshape(1, d), w_in.astype(BF16), w_out.astype(BF16), fg.reshape(1, d))


def _mem_kv_kernel(mem_ref, g_ref, wkv_ref, k_ref, v_ref, *, d):
    m = _rms(mem_ref[0], g_ref[...]).astype(BF16)
    kv = _dot(m, wkv_ref[...])
    k_ref[0] = kv[:, :d].astype(BF16)
    v_ref[0] = kv[:, d:].astype(BF16)


def _xattn_kernel(x_ref, g_ref, wq_ref, k_ref, v_ref, wo_ref, o_ref, *, heads):
    x = x_ref[0]
    d = x.shape[-1]
    hd = d // heads
    h = _rms(x, g_ref[...]).astype(BF16)
    q = (_dot(h, wq_ref[...]) * (hd ** -0.5)).astype(BF16)
    outs = []
    for i in range(heads):
        s = _dot_nt(q[:, i * hd:(i + 1) * hd], k_ref[0, :, i * hd:(i + 1) * hd])
        m = jnp.max(s, axis=-1, keepdims=True)
        p = jnp.exp(s - m)
        l = jnp.sum(p, axis=-1, keepdims=True)
        o = _dot(p.astype(BF16), v_ref[0, :, i * hd:(i + 1) * hd])
        outs.append((o / l).astype(BF16))
    o = jnp.concatenate(outs, axis=-1)
    o_ref[0] = x + _dot(o, wo_ref[...])


def _xattn(x, mem, g_x, g_mem, w_q, w_kv, w_o, tm=512):
    b, t, d = x.shape
    ml = mem.shape[1]
    k, v = pl.pallas_call(
        functools.partial(_mem_kv_kernel, d=d),
        grid=(b,),
        in_specs=[
            pl.BlockSpec((1, ml, d), lambda i: (i, 0, 0)),
            _const_spec((1, d)),
            _const_spec((d, 2 * d)),
        ],
        out_specs=[pl.BlockSpec((1, ml, d), lambda i: (i, 0, 0))] * 2,
        out_shape=[jax.ShapeDtypeStruct((b, ml, d), BF16)] * 2,
        compiler_params=_cparams("parallel"),
        name="mem_kv",
    )(mem, g_mem.reshape(1, d), w_kv.astype(BF16))
    tm = min(tm, t)
    return pl.pallas_call(
        functools.partial(_xattn_kernel, heads=X_HEADS),
        grid=(b, t // tm),
        in_specs=[
            pl.BlockSpec((1, tm, d), lambda i, j: (i, j, 0)),
            _const_spec((1, d)),
            _const_spec((d, d)),
            pl.BlockSpec((1, ml, d), lambda i, j: (i, 0, 0)),
            pl.BlockSpec((1, ml, d), lambda i, j: (i, 0, 0)),
            _const_spec((d, d)),
        ],
        out_specs=pl.BlockSpec((1, tm, d), lambda i, j: (i, j, 0)),
        out_shape=jax.ShapeDtypeStruct((b, t, d), F32),
        compiler_params=_cparams("parallel", "parallel"),
        name="xattn",
    )(x, g_x.reshape(1, d), w_q.astype(BF16), k, v, w_o.astype(BF16))


def _fold_nt_kernel(a_ref, b_ref, o_ref, *, scale):
    o = lax.dot_general(a_ref[0], b_ref[0], (((1,), (1,)), ((), ())),
                        preferred_element_type=F32, precision=lax.Precision.HIGHEST)
    o_ref[...] = (o * scale).astype(o_ref.dtype)


def _fold_nn_kernel(a_ref, b_ref, o_ref):
    o = jnp.dot(a_ref[0], b_ref[0], preferred_element_type=F32, precision=lax.Precision.HIGHEST)
    o_ref[0] = o.astype(o_ref.dtype)


def _fold_weights(w_uq, w_uk, w_uv, w_o, scale):
    nh = A_HEADS
    uq = w_uq.reshape(A_LORA, nh, A_NOPE + A_ROPE)
    uq_nope = jnp.transpose(uq[:, :, :A_NOPE], (1, 0, 2))
    w_qlat = pl.pallas_call(
        functools.partial(_fold_nt_kernel, scale=scale),
        grid=(nh,),
        in_specs=[pl.BlockSpec((1, A_LORA, A_NOPE), lambda h: (h, 0, 0)),
                  pl.BlockSpec((1, A_LORA, A_NOPE), lambda h: (h, 0, 0))],
        out_specs=pl.BlockSpec((A_LORA, A_LORA), lambda h: (0, h)),
        out_shape=jax.ShapeDtypeStruct((A_LORA, nh * A_LORA), BF16),
        compiler_params=_cparams("parallel"),
        name="fold_qlat",
    )(uq_nope, w_uk)
    d = w_o.shape[1]
    w_ov = pl.pallas_call(
        _fold_nn_kernel,
        grid=(nh,),
        in_specs=[pl.BlockSpec((1, A_LORA, A_V), lambda h: (h, 0, 0)),
                  pl.BlockSpec((1, A_V, d), lambda h: (h, 0, 0))],
        out_specs=pl.BlockSpec((1, A_LORA, d), lambda h: (h, 0, 0)),
        out_shape=jax.ShapeDtypeStruct((nh, A_LORA, d), BF16),
        compiler_params=_cparams("parallel"),
        name="fold_ov",
    )(w_uv, w_o.reshape(nh, A_V, d))
    return w_qlat, w_ov


def _dsa_proj_kernel(x_ref, pos_ref, g_ref, wa_ref, nq_ref, nkv_ref, lng_ref, lnb_ref, invc_ref,
                     wqi_ref, wpe_ref, wql_ref,
                     kv_ref, kvt_ref, kidx_ref, qidx_ref, widx_ref, q_ref):
    x = x_ref[0]
    tm = x.shape[0]
    h = _rms(x, g_ref[...]).astype(BF16)
    p = _dot(h, wa_ref[...])
    cq = _rms(p[:, 0:256], nq_ref[...])
    ckv = _rms(p[:, 256:512], nkv_ref[...])
    ang = invc_ref[...] * pos_ref[0].astype(F32)
    c32 = jnp.concatenate([jnp.cos(ang)] * 8, axis=0).T
    s32 = jnp.concatenate([jnp.sin(ang)] * 8, axis=0).T
    lane = lax.broadcasted_iota(I32, (tm, 128), 1)
    roped = (lane & (IDX_DIM - 1)) < A_ROPE
    c64 = jnp.where(roped, c32, 1.0)
    s64 = jnp.where(roped, s32, 0.0)

    g2 = p[:, 512:640]
    kr = g2 * c32 + pltpu.roll(g2, 96, 1) * s32
    kr = jnp.where(lane < A_ROPE, kr, 0.0)
    kv_ref[0, :, 0:256] = ckv.astype(BF16)
    kvt_ref[0, 0] = ckv.T.astype(BF16)
    kv_ref[0, :, 256:384] = kr.astype(BF16)

    g3 = p[:, 640:768]
    valid = lane < IDX_DIM
    mu = jnp.sum(jnp.where(valid, g3, 0.0), axis=-1, keepdims=True) * (1.0 / IDX_DIM)
    dlt = g3 - mu
    var = jnp.sum(jnp.where(valid, dlt * dlt, 0.0), axis=-1, keepdims=True) * (1.0 / IDX_DIM)
    y = dlt * lax.rsqrt(var + RMS_EPS) * lng_ref[...] + lnb_ref[...]
    ki = y * c64 + pltpu.roll(y, 64, 1) * s64
    ki = jnp.where(valid, ki, pltpu.roll(ki, 64, 1))
    ki = ki.astype(BF16)
    kidx_ref[0] = jnp.concatenate([ki, ki], axis=-1)

    g4 = p[:, 768:896] * (IDX_HEADS ** -0.5)
    widx_ref[0] = g4.T[0:8, :]

    cqb = cq.astype(BF16)
    c64x2 = jnp.concatenate([c64, c64], axis=-1)
    s64x2 = jnp.concatenate([s64, s64], axis=-1)
    qi = _dot(cqb, wqi_ref[:, 0:256]) * c64x2 + _dot(cqb, wqi_ref[:, 256:512]) * s64x2
    qidx_ref[0] = qi.astype(BF16)

    c32x4 = jnp.concatenate([c32] * 4, axis=-1)
    s32x4 = jnp.concatenate([s32] * 4, axis=-1)
    qpe = _dot(cqb, wpe_ref[:, 0:512]) * c32x4 + _dot(cqb, wpe_ref[:, 512:1024]) * s32x4
    for hh in range(A_HEADS):
        ql = _dot(cqb, wql_ref[:, hh * 256:(hh + 1) * 256])
        q_ref[0, hh, :, 0:256] = ql.astype(BF16)
        tile = qpe[:, (hh // 4) * 128:(hh // 4 + 1) * 128]
        off = (hh % 4) * A_ROPE
        if off:
            tile = pltpu.roll(tile, 128 - off, 1)
        q_ref[0, hh, :, 256:384] = jnp.where(lane < A_ROPE, tile, 0.0).astype(BF16)


def _rot_half_cols(w, width):
    k = w.shape[0]
    wg = w.reshape(k, -1, width)
    half = A_ROPE // 2
    sw = jnp.concatenate([-wg[:, :, half:A_ROPE], wg[:, :, :half],
                          jnp.zeros((k, wg.shape[1], width - A_ROPE), w.dtype)], axis=-1)
    return sw.reshape(k, -1)


def _dsa_proj(x, positions, g_mix, w_in, norm_q, norm_kv, kidx_g, kidx_b, w_uq, w_qidx, w_qlat, scale, tm=512):
    b, t, d = x.shape
    tm = min(tm, t)
    half = A_ROPE // 2
    o_kr = 2 * A_LORA
    o_ki = o_kr + A_ROPE
    o_w = o_ki + IDX_DIM
    zeros = lambda n: jnp.zeros((d, n), F32)
    w_kr = w_in[:, o_kr:o_ki]
    w_ki = w_in[:, o_ki:o_w]
    w_ki_perm = jnp.concatenate([w_ki[:, half:A_ROPE], w_ki[:, :half]], axis=-1)
    wa = jnp.concatenate([
        w_in[:, :o_kr],
        w_kr, _rot_half_cols(w_kr, A_ROPE), zeros(64),
        w_ki, w_ki_perm, zeros(32),
        w_in[:, o_w:o_w + IDX_HEADS], zeros(128 - IDX_HEADS)], axis=-1).astype(BF16)
    sgn = jnp.concatenate([-jnp.ones((half,), F32), jnp.ones((half,), F32)])
    perm = lambda v: jnp.concatenate([v[half:A_ROPE], v[:half]])
    lng = jnp.concatenate([kidx_g, sgn * perm(kidx_g), jnp.zeros((32,), F32)]).reshape(1, 128)
    lnb = jnp.concatenate([kidx_b, sgn * perm(kidx_b), jnp.zeros((32,), F32)]).reshape(1, 128)
    inv = ROPE_THETA ** (-jnp.arange(0, A_ROPE, 2, dtype=F32) / A_ROPE)
    invc = inv.reshape(A_ROPE // 2, 1)
    wqi = jnp.concatenate([w_qidx, _rot_half_cols(w_qidx, IDX_DIM)], axis=-1).astype(BF16)
    uq = w_uq.reshape(A_LORA, A_HEADS, A_NOPE + A_ROPE)
    w_pe = (uq[:, :, A_NOPE:] * scale).reshape(A_LORA, A_HEADS * A_ROPE)
    wpe = jnp.concatenate([w_pe, _rot_half_cols(w_pe, A_ROPE)], axis=-1).astype(BF16)
    row = lambda a: a.reshape(1, -1)
    outs = pl.pallas_call(
        _dsa_proj_kernel,
        grid=(b, t // tm),
        in_specs=[
            pl.BlockSpec((1, tm, d), lambda i, j: (i, j, 0)),
            pl.BlockSpec((1, 1, tm), lambda i, j: (i, 0, j)),
            _const_spec((1, d)),
            _const_spec(wa.shape),
            _const_spec((1, A_LORA)), _const_spec((1, A_LORA)),
            _const_spec((1, 128)), _const_spec((1, 128)), _const_spec((A_ROPE // 2, 1)),
            _const_spec(wqi.shape), _const_spec(wpe.shape), _const_spec(w_qlat.shape),
        ],
        out_specs=[
            pl.BlockSpec((1, tm, KV_W), lambda i, j: (i, j, 0)),
            pl.BlockSpec((1, 1, A_LORA, tm), lambda i, j: (i, j, 0, 0)),
            pl.BlockSpec((1, tm, 256), lambda i, j: (i, j, 0)),
            pl.BlockSpec((1, tm, 256), lambda i, j: (i, j, 0)),
            pl.BlockSpec((1, 8, tm), lambda i, j: (i, 0, j)),
            pl.BlockSpec((1, A_HEADS, tm, KV_W), lambda i, j: (i, 0, j, 0)),
        ],
        out_shape=[
            jax.ShapeDtypeStruct((b, t, KV_W), BF16),
            jax.ShapeDtypeStruct((b, t // tm, A_LORA, tm), BF16),
            jax.ShapeDtypeStruct((b, t, 256), BF16),
            jax.ShapeDtypeStruct((b, t, 256), BF16),
            jax.ShapeDtypeStruct((b, 8, t), F32),
            jax.ShapeDtypeStruct((b, A_HEADS, t, KV_W), BF16),
        ],
        compiler_params=_cparams("parallel", "parallel"),
        name="dsa_proj",
    )(x, positions.reshape(b, 1, t), row(g_mix), wa, row(norm_q), row(norm_kv), lng, lnb, invc,
      wqi, wpe, w_qlat)
    return outs


def _dsa_attn_kernel(x_ref, q_ref, qidx_ref, widx_ref, kidx_ref, kv_ref, kvt_ref, wov_ref, ltri_ref,
                     o_ref, keys_ref, m_ref, l_ref, acc_ref, pend_a_ref, pend_p_ref, *, tq, kb, topk):
    i = pl.program_id(1)
    t0 = i * tq
    nkb = (t0 + tq + kb - 1) // kb
    nh = A_HEADS
    rows = nh * tq

    lane_q = lax.broadcasted_iota(I32, (1, tq), 1)
    limit = ((t0 + lane_q) // CHUNK + 1) * CHUNK
    row_k = lax.broadcasted_iota(I32, (kb, tq), 0)

    qi = qidx_ref[0]
    lane256 = lax.broadcasted_iota(I32, (tq, 256), 1)
    qcat = jnp.concatenate([jnp.where((lane256 // IDX_DIM) == hh, qi, jnp.zeros_like(qi))
                            for hh in range(IDX_HEADS)], axis=0)
    wrows = [widx_ref[0, hh:hh + 1, :] for hh in range(IDX_HEADS)]

    def score_block(j, carry):
        kblk = kidx_ref[0, pl.ds(pl.multiple_of(j * kb, kb), kb), :]
        logit = _dot_nt(kblk, qcat)
        sc = jnp.zeros((kb, tq), F32)
        for hh in range(IDX_HEADS):
            sc = sc + jnp.maximum(logit[:, hh * tq:(hh + 1) * tq], 0.0) * wrows[hh]
        sc = sc * (IDX_DIM ** -0.5)
        bits = lax.bitcast_convert_type(sc, I32)
        bits = jnp.where(bits == INT_MIN, 0, bits)
        key = bits ^ ((bits >> 31) & 0x7FFFFFFF)
        adm = (j * kb + row_k) < limit
        keys_ref[j] = jnp.where(adm, key, KEY_NEG_INF)
        return carry

    lax.fori_loop(0, nkb, score_block, 0)

    def count(pred_fn):
        def body(j, c):
            hit = jnp.where(pred_fn(keys_ref[j]), 1, 0).astype(I32)
            return c + jnp.sum(hit.reshape(kb // 8, 8, tq), axis=0)
        c = lax.fori_loop(0, nkb, body, jnp.zeros((8, tq), I32))
        return jnp.sum(c, axis=0, keepdims=True)

    def bit_step(bi, thr):
        cand = thr + jnp.left_shift(jnp.int32(1), 31 - bi)
        cnt = count(lambda k: k >= cand)
        return jnp.where(cnt >= topk, cand, thr)

    thr = lax.fori_loop(0, 32, bit_step, jnp.full((1, tq), INT_MIN, I32))
    need = (topk - count(lambda k: k > thr)).astype(F32)

    m_ref[...] = jnp.full(m_ref.shape, -jnp.inf, F32)
    l_ref[...] = jnp.zeros(l_ref.shape, F32)
    acc_ref[...] = jnp.zeros(acc_ref.shape, F32)
    npair = nh // 2

    hl = A_LORA // 2

    def weighted_values(hp, alpha, pb, kvt):
        acc_ref[hp, 0:hl] = acc_ref[hp, 0:hl] * alpha + _dot(kvt[0:hl], pb)
        acc_ref[hp, hl:] = acc_ref[hp, hl:] * alpha + _dot(kvt[hl:], pb)

    def attn_block(j, tie_carry):
        key = keys_ref[j]
        gt = key > thr
        eq = key == thr
        pref = _dot(ltri_ref[...], jnp.where(eq, 1.0, 0.0).astype(BF16))
        rank = tie_carry + pref
        adm = (j * kb + row_k) < limit
        sel = adm & (gt | (eq & (rank <= need)))
        bias = jnp.where(sel, 0.0, NEG_BIG)
        bias2 = jnp.concatenate([bias, bias], axis=-1)
        kvb = kv_ref[0, pl.ds(pl.multiple_of(j * kb, kb), kb), :]
        kvt = kvt_ref[0, j]
        hk = kb // 2

        def logits(hp):
            qp = q_ref[0, 2 * hp:2 * hp + 2].reshape(2 * tq, KV_W)
            return jnp.concatenate([_dot_nt(kvb[0:hk], qp), _dot_nt(kvb[hk:kb], qp)], axis=0)

        st_next = logits(0)
        pending = (npair - 1, pend_a_ref[0:1, :], pend_p_ref[...], kvt_ref[0, jnp.maximum(j - 1, 0)])
        for hp in range(npair):
            st = st_next + bias2
            if hp + 1 < npair:
                st_next = logits(hp + 1)
            weighted_values(*pending)
            m_prev = m_ref[hp:hp + 1, :]
            m_new = jnp.maximum(m_prev, jnp.max(st, axis=0, keepdims=True))
            alpha = jnp.exp2(m_prev - m_new)
            p = jnp.exp2(st - m_new)
            l_ref[hp:hp + 1, :] = alpha * l_ref[hp:hp + 1, :] + jnp.sum(p, axis=0, keepdims=True)
            m_ref[hp:hp + 1, :] = m_new
            pending = (hp, alpha, p.astype(BF16), kvt)
        pend_a_ref[0:1, :] = pending[1]
        pend_p_ref[...] = pending[2]
        return tie_carry + pref[kb - 1:kb, :]

    pend_a_ref[...] = jnp.ones(pend_a_ref.shape, F32)
    pend_p_ref[...] = jnp.zeros(pend_p_ref.shape, BF16)
    lax.fori_loop(0, nkb, attn_block, jnp.zeros((1, tq), F32))
    weighted_values(npair - 1, pend_a_ref[0:1, :], pend_p_ref[...], kvt_ref[0, nkb - 1])

    y = x_ref[0]
    for hp in range(npair):
        o_pair = (acc_ref[hp] / l_ref[hp:hp + 1, :]).T
        for e in range(2):
            y = y + _dot(o_pair[e * tq:(e + 1) * tq].astype(BF16), wov_ref[2 * hp + e])
    o_ref[0] = y


def _dsa_attn(x, q, qidx, widx, kidx, kv, kvt, w_ov, tq=128):
    b, t, d = x.shape
    kb = kvt.shape[-1]
    topk = min(TOPK_MAX, t // 4)
    tq = min(tq, t)
    assert t % kb == 0 and t % tq == 0 and tq % CHUNK == 0
    ltri = jnp.asarray(np.tril(np.ones((kb, kb), np.float32)), BF16)
    kern = functools.partial(_dsa_attn_kernel, tq=tq, kb=kb, topk=topk)
    per_batch = lambda shape: pl.BlockSpec(shape, lambda i, j: (i,) + (0,) * (len(shape) - 1),
                                           pipeline_mode=pl.Buffered(1))
    return pl.pallas_call(
        kern,
        grid=(b, t // tq),
        in_specs=[
            pl.BlockSpec((1, tq, d), lambda i, j: (i, j, 0)),
            pl.BlockSpec((1, A_HEADS, tq, KV_W), lambda i, j: (i, 0, j, 0)),
            pl.BlockSpec((1, tq, 256), lambda i, j: (i, j, 0)),
            pl.BlockSpec((1, 8, tq), lambda i, j: (i, 0, j)),
            per_batch((1, t, 256)),
            per_batch((1, t, KV_W)),
            per_batch((1, t // kb, A_LORA, kb)),
            _const_spec(w_ov.shape),
            _const_spec((kb, kb)),
        ],
        out_specs=pl.BlockSpec((1, tq, d), lambda i, j: (i, j, 0)),
        out_shape=jax.ShapeDtypeStruct((b, t, d), F32),
        scratch_shapes=[
            pltpu.VMEM((t // kb, kb, tq), I32),
            pltpu.VMEM((A_HEADS // 2, 2 * tq), F32),
            pltpu.VMEM((A_HEADS // 2, 2 * tq), F32),
            pltpu.VMEM((A_HEADS // 2, A_LORA, 2 * tq), F32),
            pltpu.VMEM((8, 2 * tq), F32),
            pltpu.VMEM((kb, 2 * tq), BF16),
        ],
        compiler_params=_cparams("parallel", "parallel"),
        name="dsa_attn",
    )(x, q, qidx, widx, kidx, kv, kvt, w_ov, ltri)


def _dsa_layer(x, positions, g_mix, w_in, norm_q, norm_kv, kidx_g, kidx_b, w_uq, w_uk, w_uv, w_qidx, w_o):
    scale = (A_NOPE + A_ROPE) ** -0.5 * math.log2(math.e)
    w_qlat, w_ov = _fold_weights(w_uq, w_uk, w_uv, w_o, scale)
    kv, kvt, kidx, qidx, widx, q = _dsa_proj(x, positions, g_mix, w_in, norm_q, norm_kv, kidx_g, kidx_b,
                                        w_uq, w_qidx, w_qlat, scale)
    return _dsa_attn(x, q, qidx, widx, kidx, kv, kvt, w_ov)


def _gdn_proj_kernel(x_ref, g_ref, wqkv_ref, wz_ref, wba_ref, conv_ref, alog_ref, dtb_ref, lblk_ref,
                     q_ref, k_ref, v_ref, z_ref, bg_ref, buf_ref, *, tm, cw):
    @pl.when(pl.program_id(1) == 0)
    def _():
        buf_ref[0:8, :] = jnp.zeros((8, buf_ref.shape[1]), F32)

    x = x_ref[0]
    h = _rms(x, g_ref[...]).astype(BF16)
    qk_w = B_QK_HEADS * B_HEAD
    ncol = wqkv_ref.shape[1]
    for c in range(ncol // cw):
        lo = c * cw
        cur = _dot(h, wqkv_ref[:, lo:lo + cw])
        buf_ref[8:8 + tm, lo:lo + cw] = cur
        y = cur * conv_ref[3:4, lo:lo + cw]
        for tap in range(B_CONV - 1):
            sh = B_CONV - 1 - tap
            y = y + buf_ref[8 - sh:8 - sh + tm, lo:lo + cw] * conv_ref[tap:tap + 1, lo:lo + cw]
        buf_ref[0:8, lo:lo + cw] = buf_ref[tm:tm + 8, lo:lo + cw]
        y = _silu(y)
        for s in range(cw // B_HEAD):
            col = lo + s * B_HEAD
            yh = y[:, s * B_HEAD:(s + 1) * B_HEAD]
            if col < 2 * qk_w:
                yh = yh * lax.rsqrt(jnp.sum(yh * yh, axis=-1, keepdims=True) + RMS_EPS)
                if col < qk_w:
                    q_ref[0, :, col:col + B_HEAD] = (yh * (B_HEAD ** -0.5)).astype(BF16)
                else:
                    k_ref[0, :, col - qk_w:col - qk_w + B_HEAD] = yh.astype(BF16)
            else:
                v_ref[0, :, col - 2 * qk_w:col - 2 * qk_w + B_HEAD] = yh.astype(BF16)
    z_ref[0] = _dot(h, wz_ref[...]).astype(BF16)
    ba = _dot(h, wba_ref[...])
    beta = jax.nn.sigmoid(ba)
    sp_in = ba + dtb_ref[...]
    softplus = jnp.maximum(sp_in, 0.0) + jnp.log1p(jnp.exp(-jnp.abs(sp_in)))
    g = -jnp.exp(alog_ref[...]) * softplus
    gcum = jnp.dot(lblk_ref[...], g, preferred_element_type=F32, precision=lax.Precision.HIGHEST)
    lane = lax.broadcasted_iota(I32, (tm, 128), 1)
    bg_ref[0] = jnp.where(lane < B_V_HEADS, beta, gcum)


def _gdn_proj(x, g_mix, w_in, conv_w, a_log, dt_bias, tm=256):
    b, t, d = x.shape
    tm = min(tm, t)
    qk_w = B_QK_HEADS * B_HEAD
    v_w = B_V_HEADS * B_HEAD
    cwid = 2 * qk_w + v_w
    wqkv = w_in[:, :cwid].astype(BF16)
    wz = w_in[:, cwid:cwid + v_w].astype(BF16)
    wba = jnp.concatenate([w_in[:, cwid + v_w:], jnp.zeros((d, 128 - 2 * B_V_HEADS), F32)], axis=-1).astype(BF16)
    pad = lambda v: jnp.concatenate([jnp.zeros((B_V_HEADS,), F32), v, jnp.zeros((128 - 2 * B_V_HEADS,), F32)]).reshape(1, 128)
    blk = np.arange(tm) // CHUNK
    lblk = jnp.asarray(((blk[:, None] == blk[None, :]) & (np.arange(tm)[:, None] >= np.arange(tm)[None, :])).astype(np.float32))
    kern = functools.partial(_gdn_proj_kernel, tm=tm, cw=512)
    tok = lambda w: pl.BlockSpec((1, tm, w), lambda i, j: (i, j, 0))
    return pl.pallas_call(
        kern,
        grid=(b, t // tm),
        in_specs=[
            tok(d), _const_spec((1, d)), _const_spec(wqkv.shape), _const_spec(wz.shape), _const_spec(wba.shape),
            _const_spec((B_CONV, cwid)), _const_spec((1, 128)), _const_spec((1, 128)), _const_spec((tm, tm)),
        ],
        out_specs=[tok(qk_w), tok(qk_w), tok(v_w), tok(v_w), tok(128)],
        out_shape=[
            jax.ShapeDtypeStruct((b, t, qk_w), BF16),
            jax.ShapeDtypeStruct((b, t, qk_w), BF16),
            jax.ShapeDtypeStruct((b, t, v_w), BF16),
            jax.ShapeDtypeStruct((b, t, v_w), BF16),
            jax.ShapeDtypeStruct((b, t, 128), F32),
        ],
        scratch_shapes=[pltpu.VMEM((tm + 8, cwid), F32)],
        compiler_params=_cparams("parallel", "arbitrary"),
        name="gdn_proj",
    )(x, g_mix.reshape(1, d), wqkv, wz, wba, conv_w, pad(a_log), pad(dt_bias), lblk)


def _gdn_prep_kernel(q_ref, k_ref, v_ref, bg_ref, u_ref, w_ref, qg_ref, kd_ref, aqk_ref, eg_ref, *, tp):
    hv = pl.program_id(2)
    c = CHUNK
    ri = lax.broadcasted_iota(I32, (c, c), 0)
    ci = lax.broadcasted_iota(I32, (c, c), 1)
    eye = ri == ci
    lane = lax.broadcasted_iota(I32, (c, 128), 1)
    rowi = lax.broadcasted_iota(I32, (c, 1), 0)
    sls = [slice(n * c, (n + 1) * c) for n in range(tp // c)]
    bcols, gcols, decays, invs, xps = [], [], [], [], []
    for n, sl in enumerate(sls):
        bg = bg_ref[0, sl, :]
        bcol = jnp.sum(jnp.where(lane == hv, bg, 0.0), axis=-1, keepdims=True)
        gcol = jnp.sum(jnp.where(lane == hv + B_V_HEADS, bg, 0.0), axis=-1, keepdims=True)
        grow = jnp.sum(jnp.where(eye, gcol, 0.0), axis=0, keepdims=True)
        glast = jnp.sum(jnp.where(rowi == c - 1, gcol, 0.0), axis=0, keepdims=True)
        kb_ = k_ref[0, sl, :]
        qb_ = q_ref[0, sl, :]
        decay = jnp.exp(jnp.where(ri >= ci, gcol - grow, -jnp.inf))
        x_ = jnp.where(ri > ci, -(bcol * _dot_nt(kb_, kb_) * decay), 0.0)
        qg_ref[0, sl, :] = (qb_.astype(F32) * jnp.exp(gcol)).astype(BF16)
        kd_ref[0, sl, :] = (kb_.astype(F32) * jnp.exp(glast - gcol)).astype(BF16)
        aqk = jnp.where(ri >= ci, _dot_nt(qb_, kb_) * decay, 0.0)
        aqk_ref[0, sl, :] = jnp.concatenate([aqk, jnp.zeros_like(aqk)], axis=-1).astype(BF16)
        eg_ref[0, 0, n:n + 1, :] = jnp.broadcast_to(jnp.exp(glast), (1, 128))
        bcols.append(bcol)
        gcols.append(gcol)
        xps.append(x_)
        invs.append(jnp.where(eye, 1.0, 0.0) + x_)
    for _ in range(5):
        xbs = [xp.astype(BF16) for xp in xps]
        xps = [_dot(xb, xb) for xb in xbs]
        invs = [inv + _dot(inv.astype(BF16), xp.astype(BF16)) for inv, xp in zip(invs, xps)]
    for sl, inv, bcol, gcol in zip(sls, invs, bcols, gcols):
        rhs = jnp.concatenate([v_ref[0, sl, :].astype(F32) * bcol,
                               k_ref[0, sl, :].astype(F32) * (bcol * jnp.exp(gcol))], axis=-1)
        uw = _dot(inv.astype(BF16), rhs.astype(BF16))
        u_ref[0, sl, :] = uw[:, :B_HEAD]
        w_ref[0, sl, :] = uw[:, B_HEAD:].astype(BF16)


def _gdn_prep(q, k, v, bg, tp=2048):
    b, t, _ = q.shape
    tp = min(tp, t)
    nv = B_V_HEADS
    rep = B_V_HEADS // B_QK_HEADS
    v_w = nv * B_HEAD
    hspec = lambda: pl.BlockSpec((1, tp, B_HEAD), lambda i, j, h: (i, j, h))
    qkspec = lambda: pl.BlockSpec((1, tp, B_HEAD), lambda i, j, h: (i, j, h // rep))
    return pl.pallas_call(
        functools.partial(_gdn_prep_kernel, tp=tp),
        grid=(b, t // tp, nv),
        in_specs=[qkspec(), qkspec(), hspec(), pl.BlockSpec((1, tp, 128), lambda i, j, h: (i, j, 0))],
        out_specs=[hspec(), hspec(), hspec(), hspec(), hspec(),
                   pl.BlockSpec((1, 1, tp // CHUNK, 128), lambda i, j, h: (i, h, j, 0))],
        out_shape=[
            jax.ShapeDtypeStruct((b, t, v_w), F32),
            jax.ShapeDtypeStruct((b, t, v_w), BF16),
            jax.ShapeDtypeStruct((b, t, v_w), BF16),
            jax.ShapeDtypeStruct((b, t, v_w), BF16),
            jax.ShapeDtypeStruct((b, t, v_w), BF16),
            jax.ShapeDtypeStruct((b, nv, t // CHUNK, 128), F32),
        ],
        compiler_params=_cparams("parallel", "parallel", "parallel"),
        name="gdn_prep",
    )(q, k, v, bg)


def _gdn_scan_kernel(u_ref, w_ref, qg_ref, kd_ref, aqk_ref, eg_ref, z_ref, no_ref, o_ref, state_ref, *, tt, hb):
    @pl.when(pl.program_id(2) == 0)
    def _():
        state_ref[...] = jnp.zeros(state_ref.shape, F32)

    c = CHUNK

    def chunk_body(n, carry):
        r0 = pl.multiple_of(n * c, c)
        rs = pl.ds(r0, c)
        lss = [slice(hh * B_HEAD, (hh + 1) * B_HEAD) for hh in range(hb)]
        sts = [state_ref[hh] for hh in range(hb)]
        stbs = [st.astype(BF16) for st in sts]
        vbs = [(u_ref[0, rs, ls] - _dot(w_ref[0, rs, ls], stb)).astype(BF16) for ls, stb in zip(lss, stbs)]
        os_ = [_dot(qg_ref[0, rs, ls], stb) + _dot(aqk_ref[0, rs, ls][:, 0:c], vb)
               for ls, stb, vb in zip(lss, stbs, vbs)]
        for hh, (ls, st, vb) in enumerate(zip(lss, sts, vbs)):
            eg = eg_ref[0, hh, pl.ds(n, 1), :]
            upd = lax.dot_general(kd_ref[0, rs, ls], vb, (((0,), (0,)), ((), ())), preferred_element_type=F32)
            state_ref[hh] = st * eg + upd
        for ls, o in zip(lss, os_):
            on = o * lax.rsqrt(jnp.mean(o * o, axis=-1, keepdims=True) + RMS_EPS) * no_ref[...]
            o_ref[0, rs, ls] = (on * _silu(z_ref[0, rs, ls].astype(F32))).astype(BF16)
        return carry

    lax.fori_loop(0, tt // c, chunk_body, 0)


def _gdn_scan(u, w, qg, kd, aqk, eg, z, norm_o, tt=512, hb=16):
    b, t, v_w = u.shape
    tt = min(tt, t)
    nv = B_V_HEADS
    wid = hb * B_HEAD
    spec = lambda: pl.BlockSpec((1, tt, wid), lambda i, g, j: (i, j, g))
    return pl.pallas_call(
        functools.partial(_gdn_scan_kernel, tt=tt, hb=hb),
        grid=(b, nv // hb, t // tt),
        in_specs=[spec(), spec(), spec(), spec(), spec(),
                  pl.BlockSpec((1, hb, tt // CHUNK, 128), lambda i, g, j: (i, g, j, 0)),
                  spec(), _const_spec((1, B_HEAD))],
        out_specs=spec(),
        out_shape=jax.ShapeDtypeStruct((b, t, v_w), BF16),
        scratch_shapes=[pltpu.VMEM((hb, B_HEAD, B_HEAD), F32)],
        compiler_params=_cparams("parallel", "parallel", "arbitrary"),
        name="gdn_scan",
    )(u, w, qg, kd, aqk, eg, z, norm_o.reshape(1, B_HEAD))


def _proj_res_kernel(x_ref, a_ref, w_ref, o_ref):
    o_ref[...] = x_ref[...] + _dot(a_ref[...], w_ref[...])


def _proj_res(x2, a2, w, tm=512):
    n, d = x2.shape
    kdim = a2.shape[1]
    tm = min(tm, n)
    return pl.pallas_call(
        _proj_res_kernel,
        grid=(n // tm,),
        in_specs=[pl.BlockSpec((tm, d), lambda i: (i, 0)), pl.BlockSpec((tm, kdim), lambda i: (i, 0)),
                  _const_spec((kdim, d))],
        out_specs=pl.BlockSpec((tm, d), lambda i: (i, 0)),
        out_shape=jax.ShapeDtypeStruct((n, d), F32),
        compiler_params=_cparams("parallel"),
        name="proj_res",
    )(x2, a2, w.astype(BF16))


def _gdn_layer(x, g_mix, w_in, conv_w, a_log, dt_bias, norm_o, w_o):
    b, t, d = x.shape
    q, k, v, z, bg = _gdn_proj(x, g_mix, w_in, conv_w, a_log, dt_bias)
    u, w, qg, kd, aqk, eg = _gdn_prep(q, k, v, bg)
    o = _gdn_scan(u, w, qg, kd, aqk, eg, z, norm_o)
    return _proj_res(x.reshape(b * t, d), o.reshape(b * t, -1), w_o).reshape(b, t, d)


def kernel(x, mem, positions, ffn1_norm, ffn1_w_in, ffn1_w_out, mix_norm, xattn_norm, mem_norm, xattn_w_q, xattn_w_kv, xattn_w_o, ffn2_norm, ffn2_w_in, ffn2_w_out, a_w_in, a_norm_q, a_norm_kv, a_kidx_g, a_kidx_b, a_w_uq, a_w_uk, a_w_uv, a_w_qidx, a_w_o, b_w_in, b_conv, b_a_log, b_dt_bias, b_norm_o, b_w_o, final_norm):
    b, t, d = x.shape
    depth = ffn1_norm.shape[0]
    n_mixers = 2
    for i in range(depth):
        x = _ffn(x.reshape(b * t, d), ffn1_norm[i], ffn1_w_in[i], ffn1_w_out[i]).reshape(b, t, d)
        j = i // n_mixers
        if i % n_mixers == 0:
            x = _dsa_layer(x, positions, mix_norm[i], a_w_in[j], a_norm_q[j], a_norm_kv[j], a_kidx_g[j],
                           a_kidx_b[j], a_w_uq[j], a_w_uk[j], a_w_uv[j], a_w_qidx[j], a_w_o[j])
        else:
            x = _gdn_layer(x, mix_norm[i], b_w_in[j], b_conv[j], b_a_log[j], b_dt_bias[j], b_norm_o[j], b_w_o[j])
        x = _xattn(x, mem, xattn_norm[i], mem_norm[i], xattn_w_q[i], xattn_w_kv[i], xattn_w_o[i])
        last = i == depth - 1
        x = _ffn(x.reshape(b * t, d), ffn2_norm[i], ffn2_w_in[i], ffn2_w_out[i],
                 final_g=final_norm if last else None).reshape(b, t, d)
    return x
```
